```python
import jax, jax.numpy as jnp
from jax import lax
import numpy as np

D_MODEL = 2048
BATCH = 2
SEQ = 16384
DEPTH = 2

N_META = 16
BLOCK = 128
ATT_HEADS = 8
ATT_HEAD_DIM = 128
ML_HEADS = 8
ML_QK_DIM = 128
ML_V_DIM = 256
CONV_WIDTH = 4
D_FF = 7168
N_EXPERTS = 8
TOP_K = 2
MOE_BLOCK = 256
EPS = 1e-6
NEG = -1e30

ATT_W = ATT_HEADS * ATT_HEAD_DIM
ML_QK_W = ML_HEADS * ML_QK_DIM
ML_V_W = ML_HEADS * ML_V_DIM
SPLITS = (ATT_W, ATT_W, ATT_W, ATT_HEADS, ML_QK_W, ML_QK_W, ML_V_W, ML_HEADS, ML_HEADS, ML_V_W, D_MODEL, D_MODEL)
IN_W = sum(SPLITS)

kernel_name = 'fox_mlstm_gated_hybrid_moe'


def _split_points():
    return [int(v) for v in np.cumsum(SPLITS)[:-1]]


def rmsnorm(x, g):
    xf = x.astype(jnp.float32)
    y = xf * lax.rsqrt(jnp.mean(xf * xf, axis=-1, keepdims=True) + EPS)
    return (y * g.astype(jnp.float32)).astype(x.dtype)


def causal_dwconv(x, w, b):
    y = lax.conv_general_dilated(x, w[:, None, :].astype(x.dtype), window_strides=(1,),
                                 padding=((CONV_WIDTH - 1, 0),),
                                 dimension_numbers=('NWC', 'WIO', 'NWC'),
                                 feature_group_count=x.shape[-1])
    return y + b.astype(x.dtype)


def forgetting_attention(q, k, v, f_logit, valid):
    B, L, H, Dh = q.shape
    nb = L // BLOCK
    scale = Dh ** -0.5
    qf = q.astype(jnp.float32)
    kf = k.astype(jnp.float32)
    vf = v.astype(jnp.float32)
    c = jnp.cumsum(jax.nn.log_sigmoid(f_logit.astype(jnp.float32)), axis=1)
    c_k = c.transpose(0, 2, 1)
    q_blocks = qf.reshape(B, nb, BLOCK, H, Dh).swapaxes(0, 1)
    c_blocks = c_k.reshape(B, H, nb, BLOCK).transpose(2, 0, 1, 3)
    kpos = jnp.arange(L)

    def one_block(args):
        i, q_i, c_i = args
        qpos = i * BLOCK + jnp.arange(BLOCK)
        s = jnp.einsum('bqhd,bkhd->bhqk', q_i, kf) * scale
        s = s + c_i[..., :, None] - c_k[:, :, None, :]
        mask = (kpos[None, :] <= qpos[:, None]) & valid[None, :]
        s = jnp.where(mask, s, NEG)
        p = jax.nn.softmax(s, axis=-1)
        return jnp.einsum('bhqk,bkhd->bqhd', p, vf)

    out = lax.map(one_block, (jnp.arange(nb), q_blocks, c_blocks))
    return out.swapaxes(0, 1).reshape(B, L, H * Dh).astype(v.dtype)


def mlstm_chunkwise(q, k, v, i_pre, log_f):
    B, L, H, dk = q.shape
    dv = v.shape[-1]
    nc = L // BLOCK

    def to_chunks(a):
        return a.reshape((B, nc, BLOCK) + a.shape[2:]).swapaxes(0, 1)

    qf = q.astype(jnp.float32)
    kf = k.astype(jnp.float32) * dk ** -0.5
    vf = v.astype(jnp.float32)
    causal = jnp.tril(jnp.ones((BLOCK, BLOCK), dtype=bool))

    def step(carry, xs):
        C, n, m = carry
        qc, kc, vc, ic, fc = xs
        bt = jnp.cumsum(fc, axis=1).transpose(0, 2, 1)
        it = ic.transpose(0, 2, 1)
        g = bt[..., -1]
        dmat = jnp.where(causal, bt[..., :, None] - bt[..., None, :] + it[..., None, :], -jnp.inf)
        inter = bt + m[..., None]
        m_row = jnp.maximum(inter, jnp.max(dmat, axis=-1))
        w_intra = jnp.exp(dmat - m_row[..., None])
        w_inter = jnp.exp(inter - m_row)
        a = jnp.einsum('bthd,bshd->bhts', qc, kc) * w_intra
        num = (jnp.einsum('bhts,bshv->bthv', a, vc)
               + w_inter.transpose(0, 2, 1)[..., None] * jnp.einsum('bhvd,bthd->bthv', C, qc))
        den = jnp.sum(a, axis=-1) + w_inter * jnp.einsum('bhd,bthd->bht', n, qc)
        denom = jnp.maximum(jnp.abs(den), jnp.exp(-m_row)).transpose(0, 2, 1)[..., None]
        h = num / denom
        src = g[..., None] - bt + it
        m_new = jnp.maximum(g + m, jnp.max(src, axis=-1))
        w_src = jnp.exp(src - m_new[..., None])
        decay = jnp.exp(g + m - m_new)
        C_new = decay[..., None, None] * C + jnp.einsum('bhs,bshv,bshd->bhvd', w_src, vc, kc)
        n_new = decay[..., None] * n + jnp.einsum('bhs,bshd->bhd', w_src, kc)
        return (C_new, n_new, m_new), h

    init = (jnp.zeros((B, H, dv, dk), jnp.float32), jnp.zeros((B, H, dk), jnp.float32),
            jnp.zeros((B, H), jnp.float32))
    xs = (to_chunks(qf), to_chunks(kf), to_chunks(vf), to_chunks(i_pre), to_chunks(log_f))
    _, hs = lax.scan(step, init, xs)
    return hs.swapaxes(0, 1).reshape(B, L, H, dv)


def hybrid_mixer(hn, valid, w_in, att_q_norm, att_k_norm, att_f_bias, ml_conv_w, ml_conv_b,
                 ml_i_bias, ml_f_bias, ml_h_norm, w_branch_att, w_branch_ml, w_out):
    B, L, _ = hn.shape
    vmask = valid[None, :, None]
    proj = hn @ w_in
    qa, ka, va, fa, qm, km, vm, im, fm, om, ga, gm = jnp.split(proj, _split_points(), axis=-1)

    qa = rmsnorm(qa.reshape(B, L, ATT_HEADS, ATT_HEAD_DIM), att_q_norm)
    ka = rmsnorm(ka.reshape(B, L, ATT_HEADS, ATT_HEAD_DIM), att_k_norm)
    y_att = forgetting_attention(qa, ka, va.reshape(B, L, ATT_HEADS, ATT_HEAD_DIM),
                                 fa + att_f_bias.astype(fa.dtype), valid)

    qk = causal_dwconv(jnp.concatenate([qm, km], axis=-1) * vmask.astype(qm.dtype), ml_conv_w, ml_conv_b)
    qk = jax.nn.silu(qk)
    qm2, km2 = jnp.split(qk, [ML_QK_W], axis=-1)
    i_pre = jnp.where(vmask, (im + ml_i_bias.astype(im.dtype)).astype(jnp.float32), -jnp.inf)
    log_f = jnp.where(vmask, jax.nn.log_sigmoid((fm + ml_f_bias.astype(fm.dtype)).astype(jnp.float32)), 0.0)
    h_ml = mlstm_chunkwise(qm2.reshape(B, L, ML_HEADS, ML_QK_DIM), km2.reshape(B, L, ML_HEADS, ML_QK_DIM),
                           vm.reshape(B, L, ML_HEADS, ML_V_DIM), i_pre, log_f)
    h_ml = rmsnorm(h_ml, ml_h_norm.reshape(ML_HEADS, ML_V_DIM)).reshape(B, L, ML_V_W)
    y_ml = (jax.nn.sigmoid(om.astype(jnp.float32)) * h_ml).astype(hn.dtype)

    merged = (jax.nn.sigmoid(ga) * (y_att @ w_branch_att) + jax.nn.sigmoid(gm) * (y_ml @ w_branch_ml))
    return merged @ w_out


def swiglu(h, w_gate, w_up, w_down):
    return (jax.nn.silu(h @ w_gate) * (h @ w_up)) @ w_down


def moe_swiglu(h, router_w, w_gate, w_up, w_down):
    B, L, D = h.shape
    xt = h.reshape(-1, D)
    N = xt.shape[0]
    A = N * TOP_K
    logits = (xt @ router_w).astype(jnp.float32)
    top_v, top_e = lax.top_k(logits, TOP_K)
    gate = jax.nn.softmax(top_v, axis=-1)
    flat_e = top_e.reshape(-1)
    flat_tok = jnp.repeat(jnp.arange(N, dtype=jnp.int32), TOP_K)
    flat_g = gate.reshape(-1)
    order = jnp.argsort(flat_e)
    se, stok, sg = flat_e[order], flat_tok[order], flat_g[order]
    counts = jnp.bincount(flat_e, length=N_EXPERTS)
    padded = (counts + MOE_BLOCK - 1) // MOE_BLOCK * MOE_BLOCK
    start = jnp.cumsum(counts) - counts
    ends = jnp.cumsum(padded)
    pstart = ends - padded
    dest = pstart[se] + jnp.arange(A) - start[se]
    n_blocks = -(-(A + N_EXPERTS * (MOE_BLOCK - 1)) // MOE_BLOCK)
    P = n_blocks * MOE_BLOCK
    buf_tok = jnp.full((P,), N, jnp.int32).at[dest].set(stok)
    buf_g = jnp.zeros((P,), jnp.float32).at[dest].set(sg)
    block_e = jnp.minimum(jnp.searchsorted(ends, jnp.arange(n_blocks) * MOE_BLOCK, side='right'), N_EXPERTS - 1)
    x_pad = jnp.concatenate([xt, jnp.zeros((1, D), xt.dtype)], axis=0)

    def run_block(args):
        tok, e = args
        xb = x_pad[tok]
        return swiglu(xb, w_gate[e], w_up[e], w_down[e])

    yb = lax.map(run_block, (buf_tok.reshape(n_blocks, MOE_BLOCK), block_e))
    y = jnp.zeros((N + 1, D), jnp.float32).at[buf_tok].add(yb.reshape(P, D).astype(jnp.float32) * buf_g[:, None])
    return y[:N].astype(h.dtype).reshape(B, L, D)


def setup_inputs(seed: int = 0) -> dict:
    key = jax.random.key(seed)
    ks = jax.random.split(key, 24)
    n_dense = (DEPTH + 1) // 2
    n_moe = DEPTH // 2

    def nrm(k, shape, scale):
        return jax.random.normal(k, shape, jnp.float32) * scale

    return {
        'x': nrm(ks[0], (BATCH, SEQ, D_MODEL), 1.0),
        'meta_tokens': nrm(ks[1], (N_META, D_MODEL), 1.0),
        'ln_mix': 1.0 + nrm(ks[2], (DEPTH, D_MODEL), 0.02),
        'w_in': nrm(ks[3], (DEPTH, D_MODEL, IN_W), D_MODEL ** -0.5),
        'att_q_norm': 1.0 + nrm(ks[4], (DEPTH, ATT_HEAD_DIM), 0.02),
        'att_k_norm': 1.0 + nrm(ks[5], (DEPTH, ATT_HEAD_DIM), 0.02),
        'att_f_bias': jnp.linspace(1.0, 5.0, ATT_HEADS, dtype=jnp.float32)[None] + nrm(ks[6], (DEPTH, ATT_HEADS), 0.1),
        'ml_conv_w': nrm(ks[7], (DEPTH, CONV_WIDTH, 2 * ML_QK_W), CONV_WIDTH ** -0.5),
        'ml_conv_b': nrm(ks[8], (DEPTH, 2 * ML_QK_W), 0.02),
        'ml_i_bias': nrm(ks[9], (DEPTH, ML_HEADS), 0.1),
        'ml_f_bias': jnp.linspace(3.0, 6.0, ML_HEADS, dtype=jnp.float32)[None] + nrm(ks[10], (DEPTH, ML_HEADS), 0.1),
        'ml_h_norm': 1.0 + nrm(ks[11], (DEPTH, ML_V_W), 0.02),
        'w_branch_att': nrm(ks[12], (DEPTH, ATT_W, D_MODEL), ATT_W ** -0.5),
        'w_branch_ml': nrm(ks[13], (DEPTH, ML_V_W, D_MODEL), ML_V_W ** -0.5),
        'w_out': nrm(ks[14], (DEPTH, D_MODEL, D_MODEL), D_MODEL ** -0.5),
        'ln_ffn': 1.0 + nrm(ks[15], (DEPTH, D_MODEL), 0.02),
        'dense_w_gate': nrm(ks[16], (n_dense, D_MODEL, D_FF), D_MODEL ** -0.5),
        'dense_w_up': nrm(ks[17], (n_dense, D_MODEL, D_FF), D_MODEL ** -0.5),
        'dense_w_down': nrm(ks[18], (n_dense, D_FF, D_MODEL), D_FF ** -0.5),
        'router_w': nrm(ks[19], (n_moe, D_MODEL, N_EXPERTS), D_MODEL ** -0.5),
        'moe_w_gate': nrm(ks[20], (n_moe, N_EXPERTS, D_MODEL, D_FF), D_MODEL ** -0.5),
        'moe_w_up': nrm(ks[21], (n_moe, N_EXPERTS, D_MODEL, D_FF), D_MODEL ** -0.5),
        'moe_w_down': nrm(ks[22], (n_moe, N_EXPERTS, D_FF, D_MODEL), D_FF ** -0.5),
    }


def reference(x, meta_tokens, ln_mix, w_in, att_q_norm, att_k_norm, att_f_bias, ml_conv_w, ml_conv_b,
              ml_i_bias, ml_f_bias, ml_h_norm, w_branch_att, w_branch_ml, w_out, ln_ffn,
              dense_w_gate, dense_w_up, dense_w_down, router_w, moe_w_gate, moe_w_up, moe_w_down):
    B = x.shape[0]
    lead = jnp.zeros((B, BLOCK - N_META, D_MODEL), x.dtype)
    meta = jnp.broadcast_to(meta_tokens[None].astype(x.dtype), (B, N_META, D_MODEL))
    h = jnp.concatenate([lead, meta, x], axis=1)
    L = h.shape[1]
    valid = jnp.arange(L) >= (BLOCK - N_META)
    vm = valid[None, :, None].astype(h.dtype)
    for layer in range(DEPTH):
        mix = hybrid_mixer(rmsnorm(h, ln_mix[layer]), valid, w_in[layer], att_q_norm[layer], att_k_norm[layer],
                           att_f_bias[layer], ml_conv_w[layer], ml_conv_b[layer], ml_i_bias[layer],
                           ml_f_bias[layer], ml_h_norm[layer], w_branch_att[layer], w_branch_ml[layer],
                           w_out[layer])
        h = h + vm * mix
        hn = rmsnorm(h, ln_ffn[layer])
        j = layer // 2
        if layer % 2 == 0:
            f = swiglu(hn, dense_w_gate[j], dense_w_up[j], dense_w_down[j])
        else:
            f = moe_swiglu(hn, router_w[j], moe_w_gate[j], moe_w_up[j], moe_w_down[j])
        h = h + vm * f
    return h[:, BLOCK:, :]
```

```python
import functools

import jax
import jax.numpy as jnp
from jax import lax
from jax.experimental import pallas as pl
from jax.experimental.pallas import tpu as pltpu

N_META = 16
CHUNK = 128
HEADS = 8
ATT_DH = 128
ML_DK = 128
ML_DV = 256
CONV_WIDTH = 4
N_EXPERTS = 8
EPS = 1e-6
NEG = -1e30

LANE = 128
SUBLANE = 8
VMEM_BYTES_V7X = 64 * 1024 * 1024
VMEM_LIMIT = 56 * 1024 * 1024

SEQ_TILE = 512
HALO = SUBLANE

F32 = jnp.float32
BF16 = jnp.bfloat16
HIGHEST = lax.Precision.HIGHEST


def _divisor_tile(n, pref, align):
    if n <= pref:
        return n
    t = (pref // align) * align
    while t >= align:
        if n % t == 0:
            return t
        t -= align
    return n


def _cparams(sem):
    return pltpu.CompilerParams(dimension_semantics=sem, vmem_limit_bytes=VMEM_LIMIT)


def _row_valid(row0, nrows, lp, npad, nbatch):
    r = row0 + lax.broadcasted_iota(jnp.int32, (nrows, 1), 0)
    invalid = (r >= 0) & (r < npad)
    for b in range(1, nbatch):
        invalid = invalid | ((r >= b * lp) & (r < b * lp + npad))
    return jnp.logical_not(invalid)


def _log_sigmoid(x):
    return jnp.minimum(x, 0.0) - jnp.log1p(jnp.exp(-jnp.abs(x)))


def _sigmoid(x):
    return 1.0 / (1.0 + jnp.exp(-x))


def _norm_inproj_kernel(x_ref, g_ref, w_ref, wg_ref, o_ref, og_ref, xn_ref):
    @pl.when(pl.program_id(1) == 0)
    def _():
        x = x_ref[...]
        ms = jnp.mean(x * x, axis=-1, keepdims=True)
        xn = x * lax.rsqrt(ms + EPS) * g_ref[...]
        xn_ref[...] = xn.astype(BF16)
        og_ref[...] = jnp.dot(xn, wg_ref[...], preferred_element_type=F32, precision=HIGHEST)

    o_ref[...] = jnp.dot(xn_ref[...], w_ref[...], preferred_element_type=F32).astype(o_ref.dtype)


def _norm_inproj(h, gain, w_big, w_gates):
    n, d = h.shape
    wn = w_big.shape[1]
    tm = _divisor_tile(n, 1024, SUBLANE)
    tn = _divisor_tile(wn, 1024, LANE)
    return pl.pallas_call(
        _norm_inproj_kernel,
        grid=(n // tm, wn // tn),
        in_specs=[
            pl.BlockSpec((tm, d), lambda i, j: (i, 0)),
            pl.BlockSpec((1, d), lambda i, j: (0, 0)),
            pl.BlockSpec((d, tn), lambda i, j: (0, j)),
            pl.BlockSpec((d, LANE), lambda i, j: (0, 0)),
        ],
        out_specs=[
            pl.BlockSpec((tm, tn), lambda i, j: (i, j)),
            pl.BlockSpec((tm, LANE), lambda i, j: (i, 0)),
        ],
        out_shape=[jax.ShapeDtypeStruct((n, wn), BF16), jax.ShapeDtypeStruct((n, LANE), F32)],
        scratch_shapes=[pltpu.VMEM((tm, d), BF16)],
        compiler_params=_cparams(("parallel", "arbitrary")),
        name="norm_inproj",
    )(h, gain, w_big, w_gates)


def _gate_prep_kernel(g_ref, b_ref, col_ref, row_ref, carry_ref, *, npad):
    c = pl.program_id(1)

    @pl.when(c == 0)
    def _():
        carry_ref[...] = jnp.zeros_like(carry_ref)

    x = g_ref[0] + b_ref[...]
    lane = lax.broadcasted_iota(jnp.int32, (CHUNK, LANE), 1)
    pos = c * CHUNK + lax.broadcasted_iota(jnp.int32, (CHUNK, LANE), 0)
    valid = pos >= npad
    ls = _log_sigmoid(x)
    z = jnp.where(lane < HEADS, ls, jnp.where((lane < 2 * HEADS) & valid, ls, 0.0))
    r_i = lax.broadcasted_iota(jnp.int32, (CHUNK, CHUNK), 0)
    c_i = lax.broadcasted_iota(jnp.int32, (CHUNK, CHUNK), 1)
    tril = (r_i >= c_i).astype(F32)
    s = jnp.dot(tril, z, preferred_element_type=F32, precision=HIGHEST)
    s = s + jnp.where(lane < HEADS, carry_ref[0:1, :], 0.0)
    carry_ref[...] = jnp.broadcast_to(s[CHUNK - 1:CHUNK, :], carry_ref.shape)
    ipre = jnp.where(valid, x, -jnp.inf)
    col = jnp.where(lane < 2 * HEADS, s, jnp.where(lane < 3 * HEADS, ipre, 0.0))
    col_ref[0] = col
    row_ref[0] = col.T[:3 * HEADS, :]


def _gate_prep(gates3, bias, npad):
    b, lp, _ = gates3.shape
    nc = lp // CHUNK
    return pl.pallas_call(
        functools.partial(_gate_prep_kernel, npad=npad),
        grid=(b, nc),
        in_specs=[
            pl.BlockSpec((1, CHUNK, LANE), lambda bi, c: (bi, c, 0)),
            pl.BlockSpec((1, LANE), lambda bi, c: (0, 0)),
        ],
        out_specs=[
            pl.BlockSpec((1, CHUNK, LANE), lambda bi, c: (bi, c, 0)),
            pl.BlockSpec((1, 3 * HEADS, CHUNK), lambda bi, c: (bi, 0, c)),
        ],
        out_shape=[jax.ShapeDtypeStruct((b, lp, LANE), F32), jax.ShapeDtypeStruct((b, 3 * HEADS, lp), F32)],
        scratch_shapes=[pltpu.VMEM((SUBLANE, LANE), F32)],
        compiler_params=_cparams(("parallel", "arbitrary")),
        name="gate_prep",
    )(gates3, bias)


def _qk_prep_kernel(qa_ref, ka_ref, qm_ref, km_ref, qmh_ref, kmh_ref, gq_ref, gk_ref, cw_ref, cb_ref,
                    qa_o, ka_o, qm_o, kmt_o, win_ref, *, npad):
    c = pl.program_id(1)
    ml_w = HEADS * ML_DK

    for hh in range(HEADS):
        sl = slice(hh * ATT_DH, (hh + 1) * ATT_DH)
        q = qa_ref[0, :, sl].astype(F32)
        q = q * lax.rsqrt(jnp.mean(q * q, axis=-1, keepdims=True) + EPS) * gq_ref[...]
        qa_o[0, :, sl] = (q * (ATT_DH ** -0.5)).astype(BF16)
        k = ka_ref[0, :, sl].astype(F32)
        k = k * lax.rsqrt(jnp.mean(k * k, axis=-1, keepdims=True) + EPS) * gk_ref[...]
        ka_o[0, :, sl] = k.astype(BF16)

    pos = c * CHUNK + lax.broadcasted_iota(jnp.int32, (CHUNK, 1), 0)
    hpos = c * CHUNK - HALO + lax.broadcasted_iota(jnp.int32, (HALO, 1), 0)
    row_ok = pos >= npad
    halo_ok = (hpos >= npad) & (c > 0)

    def conv_silu(x_ref, xh_ref, col0):
        win_ref[0:HALO, :] = jnp.where(halo_ok, xh_ref[0].astype(F32), 0.0)
        win_ref[HALO:HALO + CHUNK, :] = jnp.where(row_ok, x_ref[0].astype(F32), 0.0)
        y = jnp.broadcast_to(cb_ref[:, col0:col0 + ml_w], (CHUNK, ml_w))
        for j in range(CONV_WIDTH):
            off = HALO - (CONV_WIDTH - 1) + j
            y = y + cw_ref[j:j + 1, col0:col0 + ml_w] * win_ref[off:off + CHUNK, :]
        return y * _sigmoid(y)

    yq = conv_silu(qm_ref, qmh_ref, 0)
    qm_o[0] = yq.astype(BF16)
    yk = conv_silu(km_ref, kmh_ref, ml_w) * (ML_DK ** -0.5)
    for hh in range(HEADS):
        sl = slice(hh * ML_DK, (hh + 1) * ML_DK)
        kmt_o[0, sl, :] = yk[:, sl].T.astype(BF16)


def _qk_prep(proj3, cols, gq, gk, conv_w, conv_b, npad):
    b, lp, _ = proj3.shape
    nc = lp // CHUNK
    w = HEADS * ATT_DH
    qa_c, ka_c, qm_c, km_c = (cols[k] // w for k in ("qa", "ka", "qm", "km"))
    hpc = CHUNK // HALO

    def main(cb):
        return pl.BlockSpec((1, CHUNK, w), lambda bi, c: (bi, c, cb))

    def halo(cb):
        return pl.BlockSpec((1, HALO, w), lambda bi, c: (bi, jnp.maximum(c * hpc - 1, 0), cb))

    full = lambda shape: pl.BlockSpec(shape, lambda bi, c: (0,) * len(shape))
    return pl.pallas_call(
        functools.partial(_qk_prep_kernel, npad=npad),
        grid=(b, nc),
        in_specs=[main(qa_c), main(ka_c), main(qm_c), main(km_c), halo(qm_c), halo(km_c),
                  full((1, ATT_DH)), full((1, ATT_DH)), full((CONV_WIDTH, 2 * w)), full((1, 2 * w))],
        out_specs=[
            pl.BlockSpec((1, CHUNK, w), lambda bi, c: (bi, c, 0)),
            pl.BlockSpec((1, CHUNK, w), lambda bi, c: (bi, c, 0)),
            pl.BlockSpec((1, CHUNK, w), lambda bi, c: (bi, c, 0)),
            pl.BlockSpec((1, w, CHUNK), lambda bi, c: (bi, 0, c)),
        ],
        out_shape=[jax.ShapeDtypeStruct((b, lp, w), BF16)] * 3 + [jax.ShapeDtypeStruct((b, w, lp), BF16)],
        scratch_shapes=[pltpu.VMEM((HALO + CHUNK, w), F32)],
        compiler_params=_cparams(("parallel", "arbitrary")),
        name="qk_prep",
    )(proj3, proj3, proj3, proj3, proj3, proj3, gq, gk, conv_w, conv_b)


def _fox_kernel(q_ref, k_ref, v_ref, c_ref, o_ref, m_ref, l_ref, acc_ref, *, tq, npad):
    i = pl.program_id(2)
    q = q_ref[0]
    c0 = c_ref[0, i][:, 0:1]
    m_ref[...] = jnp.full_like(m_ref, NEG)
    l_ref[...] = jnp.zeros_like(l_ref)
    acc_ref[...] = jnp.zeros_like(acc_ref)

    def chunk(j, masked):
        start = j * tq if isinstance(j, int) else pl.multiple_of(j * tq, tq)
        k = k_ref[0, pl.ds(start, tq), :]
        v = v_ref[0, pl.ds(start, tq), :]
        s = lax.dot_general(q, k, (((1,), (1,)), ((), ())), preferred_element_type=F32)
        s = s + (c0 - c_ref[0, j])
        if masked:
            qpos = i * tq + lax.broadcasted_iota(jnp.int32, (tq, tq), 0)
            kpos = j * tq + lax.broadcasted_iota(jnp.int32, (tq, tq), 1)
            s = jnp.where((kpos <= qpos) & (kpos >= npad), s, NEG)
        m_prev = m_ref[...]
        m_new = jnp.maximum(m_prev, jnp.max(s, axis=1, keepdims=True))
        alpha = jnp.exp(m_prev - m_new)
        p = jnp.exp(s - m_new)
        l_ref[...] = alpha * l_ref[...] + jnp.sum(p, axis=1, keepdims=True)
        acc_ref[...] = alpha * acc_ref[...] + jnp.dot(p.astype(BF16), v, preferred_element_type=F32)
        m_ref[...] = m_new

    chunk(0, True)

    def body(j, carry):
        chunk(j, False)
        return carry

    lax.fori_loop(1, i, body, 0)

    @pl.when(i > 0)
    def _():
        chunk(i, True)

    o_ref[0] = (acc_ref[...] / l_ref[...]).astype(o_ref.dtype)


def _fox_attention(qa, ka, proj3, v_col, c_rows, npad):
    b, lp, w = qa.shape
    tq = SEQ_TILE
    nq = lp // tq
    vcb = v_col // ATT_DH
    return pl.pallas_call(
        functools.partial(_fox_kernel, tq=tq, npad=npad),
        grid=(b, HEADS, nq),
        in_specs=[
            pl.BlockSpec((1, tq, ATT_DH), lambda bi, h, i: (bi, i, h)),
            pl.BlockSpec((1, lp, ATT_DH), lambda bi, h, i: (bi, 0, h)),
            pl.BlockSpec((1, lp, ATT_DH), lambda bi, h, i: (bi, 0, vcb + h)),
            pl.BlockSpec((1, nq, 1, tq), lambda bi, h, i: (bi * HEADS + h, 0, 0, 0)),
        ],
        out_specs=pl.BlockSpec((1, tq, ATT_DH), lambda bi, h, i: (bi, i, h)),
        out_shape=jax.ShapeDtypeStruct((b, lp, w), BF16),
        scratch_shapes=[pltpu.VMEM((tq, 1), F32), pltpu.VMEM((tq, 1), F32), pltpu.VMEM((tq, ATT_DH), F32)],
        compiler_params=_cparams(("parallel", "parallel", "arbitrary")),
        name="fox_attention",
    )(qa, ka, proj3, c_rows)


def _mlstm_kernel(q_ref, kt_ref, v_ref, om_ref, col_ref, row_ref, hn_ref, o_ref, ct_ref, m_ref):
    c = pl.program_id(1)

    @pl.when(c == 0)
    def _():
        ct_ref[...] = jnp.zeros_like(ct_ref)
        m_ref[...] = jnp.zeros_like(m_ref)

    t = CHUNK
    r_i = lax.broadcasted_iota(jnp.int32, (t, t), 0)
    c_i = lax.broadcasted_iota(jnp.int32, (t, t), 1)
    causal = r_i >= c_i
    ones = jnp.ones((t, LANE), BF16)

    for hh in range(HEADS):
        q = q_ref[0, :, hh * ML_DK:(hh + 1) * ML_DK]
        kt = kt_ref[0, hh * ML_DK:(hh + 1) * ML_DK, :]
        v = v_ref[0, :, hh * ML_DV:(hh + 1) * ML_DV]
        vext = jnp.concatenate([v, ones], axis=1)
        bt_c = col_ref[0, :, HEADS + hh:HEADS + hh + 1]
        bt_r = row_ref[0, HEADS + hh:HEADS + hh + 1, :]
        it_r = row_ref[0, 2 * HEADS + hh:2 * HEADS + hh + 1, :]
        g = bt_r[:, t - 1:t]
        m = m_ref[hh:hh + 1, 0:1]

        dmat = jnp.where(causal, bt_c - bt_r + it_r, -jnp.inf)
        inter = bt_c + m
        m_row = jnp.maximum(inter, jnp.max(dmat, axis=1, keepdims=True))
        w_intra = jnp.exp(dmat - m_row)
        w_inter = jnp.exp(inter - m_row)
        qk = jnp.dot(q, kt, preferred_element_type=F32)
        a = (qk * w_intra).astype(BF16)
        ct = ct_ref[hh]
        nd = (jnp.dot(a, vext, preferred_element_type=F32)
              + w_inter * jnp.dot(q, ct.astype(BF16), preferred_element_type=F32))
        num = nd[:, :ML_DV]
        den = nd[:, ML_DV:ML_DV + 1]
        denom = jnp.maximum(jnp.abs(den), jnp.exp(-m_row))
        hv = num / denom

        src = g - bt_r + it_r
        m_new = jnp.maximum(g + m, jnp.max(src, axis=1, keepdims=True))
        w_src = jnp.exp(src - m_new)
        decay = jnp.exp(g + m - m_new)
        ks = (kt.astype(F32) * w_src).astype(BF16)
        ct_ref[hh] = decay * ct + jnp.dot(ks, vext, preferred_element_type=F32)
        m_ref[hh:hh + 1, :] = jnp.broadcast_to(m_new, (1, LANE))

        hv = hv * lax.rsqrt(jnp.mean(hv * hv, axis=-1, keepdims=True) + EPS)
        hv = hv * hn_ref[:, hh * ML_DV:(hh + 1) * ML_DV]
        og = _sigmoid(om_ref[0, :, hh * ML_DV:(hh + 1) * ML_DV].astype(F32))
        o_ref[0, :, hh * ML_DV:(hh + 1) * ML_DV] = (og * hv).astype(o_ref.dtype)


def _mlstm(qm, kmt, proj3, cols, gcol, grow, hnorm):
    b, lp, qw = qm.shape
    nc = lp // CHUNK
    vw = HEADS * ML_DV
    v_cb = cols["vm"] // vw
    om_cb = cols["om"] // vw
    return pl.pallas_call(
        _mlstm_kernel,
        grid=(b, nc),
        in_specs=[
            pl.BlockSpec((1, CHUNK, qw), lambda bi, c: (bi, c, 0)),
            pl.BlockSpec((1, qw, CHUNK), lambda bi, c: (bi, 0, c)),
            pl.BlockSpec((1, CHUNK, vw), lambda bi, c: (bi, c, v_cb)),
            pl.BlockSpec((1, CHUNK, vw), lambda bi, c: (bi, c, om_cb)),
            pl.BlockSpec((1, CHUNK, LANE), lambda bi, c: (bi, c, 0)),
            pl.BlockSpec((1, 3 * HEADS, CHUNK), lambda bi, c: (bi, 0, c)),
            pl.BlockSpec((1, vw), lambda bi, c: (0, 0)),
        ],
        out_specs=pl.BlockSpec((1, CHUNK, vw), lambda bi, c: (bi, c, 0)),
        out_shape=jax.ShapeDtypeStruct((b, lp, vw), BF16),
        scratch_shapes=[pltpu.VMEM((HEADS, ML_DK, ML_DV + LANE), F32), pltpu.VMEM((HEADS, LANE), F32)],
        compiler_params=_cparams(("parallel", "arbitrary")),
        name="mlstm",
    )(qm, kmt, proj3, proj3, gcol, grow, hnorm)


def _merge_kernel(ya_ref, ym_ref, wa_ref, wm_ref, ga_ref, gm_ref, o_ref):
    a = jnp.dot(ya_ref[...], wa_ref[...], preferred_element_type=F32)
    m = jnp.dot(ym_ref[...], wm_ref[...], preferred_element_type=F32)
    out = _sigmoid(ga_ref[...].astype(F32)) * a + _sigmoid(gm_ref[...].astype(F32)) * m
    o_ref[...] = out.astype(o_ref.dtype)


def _merge(y_att, y_ml, wa, wm, proj, cols):
    n, ka = y_att.shape
    km = y_ml.shape[1]
    d = wa.shape[1]
    tm = _divisor_tile(n, 1024, SUBLANE)
    tn = _divisor_tile(d, 1024, LANE)
    ga_cb = cols["ga"] // tn
    gm_cb = cols["gm"] // tn
    return pl.pallas_call(
        _merge_kernel,
        grid=(n // tm, d // tn),
        in_specs=[
            pl.BlockSpec((tm, ka), lambda i, j: (i, 0)),
            pl.BlockSpec((tm, km), lambda i, j: (i, 0)),
            pl.BlockSpec((ka, tn), lambda i, j: (0, j)),
            pl.BlockSpec((km, tn), lambda i, j: (0, j)),
            pl.BlockSpec((tm, tn), lambda i, j: (i, ga_cb + j)),
            pl.BlockSpec((tm, tn), lambda i, j: (i, gm_cb + j)),
        ],
        out_specs=pl.BlockSpec((tm, tn), lambda i, j: (i, j)),
        out_shape=jax.ShapeDtypeStruct((n, d), BF16),
        compiler_params=_cparams(("parallel", "arbitrary")),
        name="merge",
    )(y_att, y_ml, wa, wm, proj, proj)


def _out_residual_kernel(x_ref, w_ref, h_ref, o_ref, *, tm, lp, npad, nbatch):
    mix = jnp.dot(x_ref[...], w_ref[...], preferred_element_type=F32)
    valid = _row_valid(pl.program_id(0) * tm, tm, lp, npad, nbatch)
    o_ref[...] = h_ref[...] + jnp.where(valid, mix, 0.0)


def _out_residual(x, w, h, lp, npad, nbatch):
    n, k = x.shape
    d = w.shape[1]
    tm = _divisor_tile(n, 1024, SUBLANE)
    tn = _divisor_tile(d, 1024, LANE)
    return pl.pallas_call(
        functools.partial(_out_residual_kernel, tm=tm, lp=lp, npad=npad, nbatch=nbatch),
        grid=(n // tm, d // tn),
        in_specs=[
            pl.BlockSpec((tm, k), lambda i, j: (i, 0)),
            pl.BlockSpec((k, tn), lambda i, j: (0, j)),
            pl.BlockSpec((tm, tn), lambda i, j: (i, j)),
        ],
        out_specs=pl.BlockSpec((tm, tn), lambda i, j: (i, j)),
        out_shape=jax.ShapeDtypeStruct((n, d), F32),
        compiler_params=_cparams(("parallel", "arbitrary")),
        name="out_residual",
    )(x, w, h)


def _ffn_kernel(x_ref, g_ref, wg_ref, wu_ref, wd_ref, o_ref, xn_ref, *, tm, lp, npad, nbatch):
    f = pl.program_id(1)

    @pl.when(f == 0)
    def _():
        x = x_ref[...]
        ms = jnp.mean(x * x, axis=-1, keepdims=True)
        xn_ref[...] = (x * lax.rsqrt(ms + EPS) * g_ref[...]).astype(BF16)
        o_ref[...] = jnp.zeros_like(o_ref)

    xn = xn_ref[...]
    gate = jnp.dot(xn, wg_ref[...], preferred_element_type=F32)
    up = jnp.dot(xn, wu_ref[...], preferred_element_type=F32)
    act = (gate * _sigmoid(gate) * up).astype(BF16)
    o_ref[...] += jnp.dot(act, wd_ref[...], preferred_element_type=F32)

    @pl.when(f == pl.num_programs(1) - 1)
    def _():
        valid = _row_valid(pl.program_id(0) * tm, tm, lp, npad, nbatch)
        o_ref[...] = x_ref[...] + jnp.where(valid, o_ref[...], 0.0)


def _ffn(h, gain, wg, wu, wd, lp, npad, nbatch):
    n, d = h.shape
    dff = wg.shape[1]
    tm = _divisor_tile(n, 1024, SUBLANE)
    tf = _divisor_tile(dff, 512, LANE)
    return pl.pallas_call(
        functools.partial(_ffn_kernel, tm=tm, lp=lp, npad=npad, nbatch=nbatch),
        grid=(n // tm, dff // tf),
        in_specs=[
            pl.BlockSpec((tm, d), lambda i, f: (i, 0)),
            pl.BlockSpec((1, d), lambda i, f: (0, 0)),
            pl.BlockSpec((d, tf), lambda i, f: (0, f)),
            pl.BlockSpec((d, tf), lambda i, f: (0, f)),
            pl.BlockSpec((tf, d), lambda i, f: (f, 0)),
        ],
        out_specs=pl.BlockSpec((tm, d), lambda i, f: (i, 0)),
        out_shape=jax.ShapeDtypeStruct((n, d), F32),
        scratch_shapes=[pltpu.VMEM((tm, d), BF16)],
        compiler_params=_cparams(("parallel", "arbitrary")),
        name="ffn",
    )(h, gain, wg, wu, wd)


def _bf16_bits(x):
    bits = lax.bitcast_convert_type(x, jnp.uint32)
    rounded = bits + jnp.uint32(0x7FFF) + ((bits >> 16) & jnp.uint32(1))
    return rounded >> 16


def _router_kernel(x_ref, g_ref, rw_ref, xp_ref, route_ref):
    x = x_ref[...]
    d = x.shape[1]
    ms = jnp.mean(x * x, axis=-1, keepdims=True)
    xn = x * lax.rsqrt(ms + EPS) * g_ref[...]
    xp_ref[...] = _bf16_bits(xn[:, :d // 2]) | (_bf16_bits(xn[:, d // 2:]) << 16)

    logits = jnp.dot(xn, rw_ref[...], preferred_element_type=F32, precision=HIGHEST)
    lane = lax.broadcasted_iota(jnp.int32, logits.shape, 1)
    logits = jnp.where(lane < N_EXPERTS, logits, -jnp.inf)
    v1 = jnp.max(logits, axis=-1, keepdims=True)
    i1 = jnp.min(jnp.where(logits == v1, lane, LANE), axis=-1, keepdims=True)
    rest = jnp.where(lane == i1, -jnp.inf, logits)
    v2 = jnp.max(rest, axis=-1, keepdims=True)
    i2 = jnp.min(jnp.where(rest == v2, lane, LANE), axis=-1, keepdims=True)
    e = jnp.exp(v2 - v1)
    g1 = 1.0 / (1.0 + e)
    g2 = e / (1.0 + e)
    route = jnp.where(lane == 0, i1.astype(F32),
                      jnp.where(lane == 1, i2.astype(F32),
                                jnp.where(lane == 2, g1, jnp.where(lane == 3, g2, 0.0))))
    route_ref[...] = route


def _router(h, gain, rw):
    n, d = h.shape
    tm = _divisor_tile(n, 1024, SUBLANE)
    return pl.pallas_call(
        _router_kernel,
        grid=(n // tm,),
        in_specs=[
            pl.BlockSpec((tm, d), lambda i: (i, 0)),
            pl.BlockSpec((1, d), lambda i: (0, 0)),
            pl.BlockSpec((d, LANE), lambda i: (0, 0)),
        ],
        out_specs=[
            pl.BlockSpec((tm, d // 2), lambda i: (i, 0)),
            pl.BlockSpec((tm, LANE), lambda i: (i, 0)),
        ],
        out_shape=[jax.ShapeDtypeStruct((n, d // 2), jnp.uint32), jax.ShapeDtypeStruct((n, LANE), F32)],
        compiler_params=_cparams(("parallel",)),
        name="router",
    )(h, gain, rw)


GATHER_ROWS = 512


def _row_copy(src_hbm, src_row, dst_ref, dst_row, sem):
    return pltpu.make_async_copy(src_hbm.at[pl.ds(src_row, 1)], dst_ref.at[pl.ds(dst_row, 1)], sem)


def _gather_kernel(idx_ref, x_hbm, o_ref, sem, *, rows):
    def start(r, carry):
        _row_copy(x_hbm, idx_ref[0, 0, r], o_ref, r, sem).start()
        return carry

    lax.fori_loop(0, rows, start, 0)

    def wait(r, carry):
        _row_copy(x_hbm, 0, o_ref, r, sem).wait()
        return carry

    lax.fori_loop(0, rows, wait, 0)


def _moe_gather(xp, slot_tok, rows):
    p = slot_tok.shape[0]
    w = xp.shape[1]
    nb = p // rows
    idx3 = slot_tok.reshape(nb, 1, rows)
    return pl.pallas_call(
        functools.partial(_gather_kernel, rows=rows),
        grid=(nb,),
        in_specs=[
            pl.BlockSpec((1, 1, rows), lambda i: (i, 0, 0), memory_space=pltpu.SMEM),
            pl.BlockSpec(memory_space=pl.ANY),
        ],
        out_specs=pl.BlockSpec((rows, w), lambda i: (i, 0)),
        out_shape=jax.ShapeDtypeStruct((p, w), xp.dtype),
        scratch_shapes=[pltpu.SemaphoreType.DMA(())],
        compiler_params=_cparams(("arbitrary",)),
        name="moe_gather",
    )(idx3, xp)


def _moe_ffn_kernel(blk_e_ref, nblk_ref, xg_ref, wg_ref, wu_ref, wd_ref, o_ref, xn_ref):
    i = pl.program_id(0)
    f = pl.program_id(1)

    @pl.when(i < nblk_ref[0])
    def _():
        @pl.when(f == 0)
        def _():
            packed = xg_ref[...]
            half = packed.shape[1]
            lo = lax.bitcast_convert_type(packed << 16, F32)
            hi = lax.bitcast_convert_type(packed & jnp.uint32(0xFFFF0000), F32)
            xn_ref[:, :half] = lo.astype(BF16)
            xn_ref[:, half:] = hi.astype(BF16)
            o_ref[...] = jnp.zeros_like(o_ref)

        xn = xn_ref[...]
        gate = jnp.dot(xn, wg_ref[0], preferred_element_type=F32)
        up = jnp.dot(xn, wu_ref[0], preferred_element_type=F32)
        act = (gate * _sigmoid(gate) * up).astype(BF16)
        o_ref[...] += jnp.dot(act, wd_ref[0], preferred_element_type=F32)

    @pl.when((i >= nblk_ref[0]) & (f == 0))
    def _():
        o_ref[...] = jnp.zeros_like(o_ref)


def _moe_ffn(xg, blk_e, nblk, wg, wu, wd, tm):
    p, half = xg.shape
    d = 2 * half
    dff = wg.shape[2]
    tf = _divisor_tile(dff, 512, LANE)
    nf = dff // tf
    nb = p // tm

    def row_map(i, f, be, nu):
        return (jnp.minimum(i, nu[0] - 1), 0)

    def f_eff(i, f, nu):
        return jnp.where(i < nu[0], f, nf - 1)

    grid_spec = pltpu.PrefetchScalarGridSpec(
        num_scalar_prefetch=2,
        grid=(nb, nf),
        in_specs=[
            pl.BlockSpec((tm, half), row_map),
            pl.BlockSpec((1, d, tf), lambda i, f, be, nu: (be[i], 0, f_eff(i, f, nu))),
            pl.BlockSpec((1, d, tf), lambda i, f, be, nu: (be[i], 0, f_eff(i, f, nu))),
            pl.BlockSpec((1, tf, d), lambda i, f, be, nu: (be[i], f_eff(i, f, nu), 0)),
        ],
        out_specs=pl.BlockSpec((tm, d), lambda i, f, be, nu: (i, 0)),
        scratch_shapes=[pltpu.VMEM((tm, d), BF16)],
    )
    return pl.pallas_call(
        _moe_ffn_kernel,
        grid_spec=grid_spec,
        out_shape=jax.ShapeDtypeStruct((p, d), F32),
        compiler_params=_cparams(("arbitrary", "arbitrary")),
        name="moe_ffn",
    )(blk_e, nblk, xg, wg, wu, wd)


def _combine_kernel(d_ref, route_ref, h_ref, y_hbm, o_ref, y0_ref, y1_ref, sem, *, rows, lp, npad, nbatch):
    def start(r, carry):
        _row_copy(y_hbm, d_ref[0, 0, 2 * r], y0_ref, r, sem).start()
        _row_copy(y_hbm, d_ref[0, 0, 2 * r + 1], y1_ref, r, sem).start()
        return carry

    lax.fori_loop(0, rows, start, 0)

    def wait(r, carry):
        _row_copy(y_hbm, 0, y0_ref, r, sem).wait()
        _row_copy(y_hbm, 0, y1_ref, r, sem).wait()
        return carry

    lax.fori_loop(0, rows, wait, 0)

    g0 = route_ref[:, 2:3]
    g1 = route_ref[:, 3:4]
    y = g0 * y0_ref[...] + g1 * y1_ref[...]
    valid = _row_valid(pl.program_id(0) * rows, rows, lp, npad, nbatch)
    o_ref[...] = h_ref[...] + jnp.where(valid, y, 0.0)


def _moe_combine(h, route, dest, yg, rows, lp, npad, nbatch):
    n, d = h.shape
    nb = n // rows
    d3 = dest.reshape(nb, 1, 2 * rows)
    return pl.pallas_call(
        functools.partial(_combine_kernel, rows=rows, lp=lp, npad=npad, nbatch=nbatch),
        grid=(nb,),
        in_specs=[
            pl.BlockSpec((1, 1, 2 * rows), lambda i: (i, 0, 0), memory_space=pltpu.SMEM),
            pl.BlockSpec((rows, LANE), lambda i: (i, 0)),
            pl.BlockSpec((rows, d), lambda i: (i, 0)),
            pl.BlockSpec(memory_space=pl.ANY),
        ],
        out_specs=pl.BlockSpec((rows, d), lambda i: (i, 0)),
        out_shape=jax.ShapeDtypeStruct((n, d), F32),
        scratch_shapes=[pltpu.VMEM((rows, d), F32), pltpu.VMEM((rows, d), F32), pltpu.SemaphoreType.DMA(())],
        compiler_params=_cparams(("arbitrary",)),
        name="moe_combine",
    )(d3, route, h, yg)


def _moe_layout(experts, tm, nblocks):
    n = experts.shape[0]
    flat_e = experts.reshape(-1)
    onehot = (flat_e[:, None] == jnp.arange(N_EXPERTS, dtype=jnp.int32)[None, :]).astype(jnp.int32)
    csum = jnp.cumsum(onehot, axis=0)
    pos = jnp.take_along_axis(csum, flat_e[:, None], axis=1)[:, 0] - 1
    counts = csum[-1]
    padded = (counts + tm - 1) // tm * tm
    ends = jnp.cumsum(padded)
    gstart = ends - padded
    dest = (gstart[flat_e] + pos).astype(jnp.int32)
    flat_tok = jnp.arange(2 * n, dtype=jnp.int32) // 2
    slot_tok = jnp.zeros((nblocks * tm,), jnp.int32).at[dest].set(flat_tok)
    nblk = (ends[-1] // tm).astype(jnp.int32)
    blk = jnp.minimum(jnp.arange(nblocks, dtype=jnp.int32), nblk - 1)
    blk_e = jnp.minimum(jnp.searchsorted(ends, blk * tm, side="right"), N_EXPERTS - 1).astype(jnp.int32)
    return dest, slot_tok, blk_e, nblk.reshape(1)


def _moe_block(h, gain, rw, wg, wu, wd, lp, npad, nbatch):
    n, d = h.shape
    xp, route = _router(h, gain, rw)
    experts = route[:, :2].astype(jnp.int32)
    tm = _divisor_tile(n, 1024, GATHER_ROWS) if n % GATHER_ROWS == 0 else n
    rows = min(GATHER_ROWS, tm)
    nblocks = -(-(2 * n + N_EXPERTS * (tm - 1)) // tm)
    dest, slot_tok, blk_e, nblk = _moe_layout(experts, tm, nblocks)
    xg = _moe_gather(xp, slot_tok, rows)
    yg = _moe_ffn(xg, blk_e, nblk, wg, wu, wd, tm)
    return _moe_combine(h, route, dest, yg, rows, lp, npad, nbatch)


def _inproj_layout(d_model):
    att_w = HEADS * ATT_DH
    qk_w = HEADS * ML_DK
    v_w = HEADS * ML_DV
    names = ("qa", "ka", "va", "fa", "qm", "km", "vm", "im", "fm", "om", "ga", "gm")
    widths = (att_w, att_w, att_w, HEADS, qk_w, qk_w, v_w, HEADS, HEADS, v_w, d_model, d_model)
    src, off = {}, 0
    for nm, wd in zip(names, widths):
        src[nm] = (off, wd)
        off += wd
    big_order = ("vm", "om", "ga", "gm", "qa", "ka", "va", "qm", "km")
    cols, off = {}, 0
    for nm in big_order:
        cols[nm] = off
        off += src[nm][1]
    return src, big_order, cols


def kernel(x, meta_tokens, ln_mix, w_in, att_q_norm, att_k_norm, att_f_bias, ml_conv_w, ml_conv_b, ml_i_bias,
           ml_f_bias, ml_h_norm, w_branch_att, w_branch_ml, w_out, ln_ffn, dense_w_gate, dense_w_up, dense_w_down,
           router_w, moe_w_gate, moe_w_up, moe_w_down):
    nbatch, seq, d = x.shape
    depth = w_in.shape[0]
    lp = -(-(CHUNK + seq) // SEQ_TILE) * SEQ_TILE
    npad = lp - seq - N_META
    n = nbatch * lp
    src, big_order, cols = _inproj_layout(d)

    h = jnp.concatenate([
        jnp.zeros((nbatch, npad, d), x.dtype),
        jnp.broadcast_to(meta_tokens[None].astype(x.dtype), (nbatch, N_META, d)),
        x], axis=1).reshape(n, d)

    for layer in range(depth):
        wl = w_in[layer]
        w_big = jnp.concatenate([wl[:, src[nm][0]:src[nm][0] + src[nm][1]] for nm in big_order], axis=1).astype(BF16)
        w_gates = jnp.concatenate(
            [wl[:, src[nm][0]:src[nm][0] + HEADS] for nm in ("fa", "fm", "im")]
            + [jnp.zeros((d, LANE - 3 * HEADS), F32)], axis=1)
        gate_bias = jnp.concatenate([att_f_bias[layer], ml_f_bias[layer], ml_i_bias[layer],
                                     jnp.zeros((LANE - 3 * HEADS,), F32)]).reshape(1, LANE)

        proj, gates = _norm_inproj(h, ln_mix[layer].reshape(1, d), w_big, w_gates)
        proj3 = proj.reshape(nbatch, lp, -1)
        gcol, grow = _gate_prep(gates.reshape(nbatch, lp, LANE), gate_bias, npad)
        qa, ka, qm, kmt = _qk_prep(proj3, cols, att_q_norm[layer].reshape(1, ATT_DH),
                                   att_k_norm[layer].reshape(1, ATT_DH), ml_conv_w[layer],
                                   ml_conv_b[layer].reshape(1, -1), npad)
        c_rows = grow[:, :HEADS, :].reshape(nbatch * HEADS, lp // SEQ_TILE, 1, SEQ_TILE)
        y_att = _fox_attention(qa, ka, proj3, cols["va"], c_rows, npad)
        y_ml = _mlstm(qm, kmt, proj3, cols, gcol, grow, ml_h_norm[layer].reshape(1, -1))
        merged = _merge(y_att.reshape(n, -1), y_ml.reshape(n, -1), w_branch_att[layer].astype(BF16),
                        w_branch_ml[layer].astype(BF16), proj, cols)
        h = _out_residual(merged, w_out[layer].astype(BF16), h, lp, npad, nbatch)

        j = layer // 2
        gain = ln_ffn[layer].reshape(1, d)
        if layer % 2 == 0:
            h = _ffn(h, gain, dense_w_gate[j].astype(BF16), dense_w_up[j].astype(BF16),
                     dense_w_down[j].astype(BF16), lp, npad, nbatch)
        else:
            rw = jnp.concatenate([router_w[j], jnp.zeros((d, LANE - N_EXPERTS), F32)], axis=1)
            h = _moe_block(h, gain, rw, moe_w_gate[j].astype(BF16), moe_w_up[j].astype(BF16),
                           moe_w_down[j].astype(BF16), lp, npad, nbatch)

    return h.reshape(nbatch, lp, d)[:, lp - seq:, :]
```

```python
import functools

import jax
import jax.numpy as jnp
from jax import lax
from jax.experimental import pallas as pl
from jax.experimental.pallas import tpu as pltpu

N_META = 16
CHUNK = 128
HEADS = 8
ATT_DH = 128
ML_DK = 128
ML_DV = 256
CONV_WIDTH = 4
N_EXPERTS = 8
EPS = 1e-6
NEG = -1e30

LANE = 128
SUBLANE = 8
VMEM_BYTES_V7X = 64 * 1024 * 1024
VMEM_LIMIT = 56 * 1024 * 1024

SEQ_TILE = 512
HALO = SUBLANE

F32 = jnp.float32
BF16 = jnp.bfloat16
HIGHEST = lax.Precision.HIGHEST


def _divisor_tile(n, pref, align):
    if n <= pref:
        return n
    t = (pref // align) * align
    while t >= align:
        if n % t == 0:
            return t
        t -= align
    return n


def _cparams(sem):
    return pltpu.CompilerParams(dimension_semantics=sem, vmem_limit_bytes=VMEM_LIMIT)


def _row_valid(row0, nrows, lp, npad, nbatch):
    r = row0 + lax.broadcasted_iota(jnp.int32, (nrows, 1), 0)
    invalid = (r >= 0) & (r < npad)
    for b in range(1, nbatch):
        invalid = invalid | ((r >= b * lp) & (r < b * lp + npad))
    return jnp.logical_not(invalid)


def _log_sigmoid(x):
    return jnp.minimum(x, 0.0) - jnp.log1p(jnp.exp(-jnp.abs(x)))


def _sigmoid(x):
    return 1.0 / (1.0 + jnp.exp(-x))


def _norm_inproj_kernel(x_ref, g_ref, w_ref, wg_ref, o_ref, og_ref, xn_ref):
    @pl.when(pl.program_id(1) == 0)
    def _():
        x = x_ref[...]
        ms = jnp.mean(x * x, axis=-1, keepdims=True)
        xn = x * lax.rsqrt(ms + EPS) * g_ref[...]
        xn_ref[...] = xn.astype(BF16)
        og_ref[...] = jnp.dot(xn, wg_ref[...], preferred_element_type=F32, precision=HIGHEST)

    o_ref[...] = jnp.dot(xn_ref[...], w_ref[...], preferred_element_type=F32).astype(o_ref.dtype)


def _norm_inproj(h, gain, w_big, w_gates):
    n, d = h.shape
    wn = w_big.shape[1]
    tm = _divisor_tile(n, 1024, SUBLANE)
    tn = _divisor_tile(wn, 1024, LANE)
    return pl.pallas_call(
        _norm_inproj_kernel,
        grid=(n // tm, wn // tn),
        in_specs=[
            pl.BlockSpec((tm, d), lambda i, j: (i, 0)),
            pl.BlockSpec((1, d), lambda i, j: (0, 0)),
            pl.BlockSpec((d, tn), lambda i, j: (0, j)),
            pl.BlockSpec((d, LANE), lambda i, j: (0, 0)),
        ],
        out_specs=[
            pl.BlockSpec((tm, tn), lambda i, j: (i, j)),
            pl.BlockSpec((tm, LANE), lambda i, j: (i, 0)),
        ],
        out_shape=[jax.ShapeDtypeStruct((n, wn), BF16), jax.ShapeDtypeStruct((n, LANE), F32)],
        scratch_shapes=[pltpu.VMEM((tm, d), BF16)],
        compiler_params=_cparams(("parallel", "arbitrary")),
        name="norm_inproj",
    )(h, gain, w_big, w_gates)


def _gate_prep_kernel(g_ref, b_ref, col_ref, row_ref, carry_ref, *, npad):
    c = pl.program_id(1)

    @pl.when(c == 0)
    def _():
        carry_ref[...] = jnp.zeros_like(carry_ref)

    x = g_ref[0] + b_ref[...]
    lane = lax.broadcasted_iota(jnp.int32, (CHUNK, LANE), 1)
    pos = c * CHUNK + lax.broadcasted_iota(jnp.int32, (CHUNK, LANE), 0)
    valid = pos >= npad
    ls = _log_sigmoid(x)
    z = jnp.where(lane < HEADS, ls, jnp.where((lane < 2 * HEADS) & valid, ls, 0.0))
    r_i = lax.broadcasted_iota(jnp.int32, (CHUNK, CHUNK), 0)
    c_i = lax.broadcasted_iota(jnp.int32, (CHUNK, CHUNK), 1)
    tril = (r_i >= c_i).astype(F32)
    s = jnp.dot(tril, z, preferred_element_type=F32, precision=HIGHEST)
    s = s + jnp.where(lane < HEADS, carry_ref[0:1, :], 0.0)
    carry_ref[...] = jnp.broadcast_to(s[CHUNK - 1:CHUNK, :], carry_ref.shape)
    ipre = jnp.where(valid, x, -jnp.inf)
    col = jnp.where(lane < 2 * HEADS, s, jnp.where(lane < 3 * HEADS, ipre, 0.0))
    col_ref[0] = col
    row_ref[0] = col.T[:3 * HEADS, :]


def _gate_prep(gates3, bias, npad):
    b, lp, _ = gates3.shape
    nc = lp // CHUNK
    return pl.pallas_call(
        functools.partial(_gate_prep_kernel, npad=npad),
        grid=(b, nc),
        in_specs=[
            pl.BlockSpec((1, CHUNK, LANE), lambda bi, c: (bi, c, 0)),
            pl.BlockSpec((1, LANE), lambda bi, c: (0, 0)),
        ],
        out_specs=[
            pl.BlockSpec((1, CHUNK, LANE), lambda bi, c: (bi, c, 0)),
            pl.BlockSpec((1, 3 * HEADS, CHUNK), lambda bi, c: (bi, 0, c)),
        ],
        out_shape=[jax.ShapeDtypeStruct((b, lp, LANE), F32), jax.ShapeDtypeStruct((b, 3 * HEADS, lp), F32)],
        scratch_shapes=[pltpu.VMEM((SUBLANE, LANE), F32)],
        compiler_params=_cparams(("parallel", "arbitrary")),
        name="gate_prep",
    )(gates3, bias)


def _qk_prep_kernel(qa_ref, ka_ref, qm_ref, km_ref, qmh_ref, kmh_ref, gq_ref, gk_ref, cw_ref, cb_ref,
                    qa_o, ka_o, qm_o, kmt_o, win_ref, *, npad):
    c = pl.program_id(1)
    ml_w = HEADS * ML_DK

    for hh in range(HEADS):
        sl = slice(hh * ATT_DH, (hh + 1) * ATT_DH)
        q = qa_ref[0, :, sl].astype(F32)
        q = q * lax.rsqrt(jnp.mean(q * q, axis=-1, keepdims=True) + EPS) * gq_ref[...]
        qa_o[0, :, sl] = (q * (ATT_DH ** -0.5)).astype(BF16)
        k = ka_ref[0, :, sl].astype(F32)
        k = k * lax.rsqrt(jnp.mean(k * k, axis=-1, keepdims=True) + EPS) * gk_ref[...]
        ka_o[0, :, sl] = k.astype(BF16)

    pos = c * CHUNK + lax.broadcasted_iota(jnp.int32, (CHUNK, 1), 0)
    hpos = c * CHUNK - HALO + lax.broadcasted_iota(jnp.int32, (HALO, 1), 0)
    row_ok = pos >= npad
    halo_ok = (hpos >= npad) & (c > 0)

    def conv_silu(x_ref, xh_ref, col0):
        win_ref[0:HALO, :] = jnp.where(halo_ok, xh_ref[0].astype(F32), 0.0)
        win_ref[HALO:HALO + CHUNK, :] = jnp.where(row_ok, x_ref[0].astype(F32), 0.0)
        y = jnp.broadcast_to(cb_ref[:, col0:col0 + ml_w], (CHUNK, ml_w))
        for j in range(CONV_WIDTH):
            off = HALO - (CONV_WIDTH - 1) + j
            y = y + cw_ref[j:j + 1, col0:col0 + ml_w] * win_ref[off:off + CHUNK, :]
        return y * _sigmoid(y)

    yq = conv_silu(qm_ref, qmh_ref, 0)
    qm_o[0] = yq.astype(BF16)
    yk = conv_silu(km_ref, kmh_ref, ml_w) * (ML_DK ** -0.5)
    for hh in range(HEADS):
        sl = slice(hh * ML_DK, (hh + 1) * ML_DK)
        kmt_o[0, sl, :] = yk[:, sl].T.astype(BF16)


def _qk_prep(proj3, cols, gq, gk, conv_w, conv_b, npad):
    b, lp, _ = proj3.shape
    nc = lp // CHUNK
    w = HEADS * ATT_DH
    qa_c, ka_c, qm_c, km_c = (cols[k] // w for k in ("qa", "ka", "qm", "km"))
    hpc = CHUNK // HALO

    def main(cb):
        return pl.BlockSpec((1, CHUNK, w), lambda bi, c: (bi, c, cb))

    def halo(cb):
        return pl.BlockSpec((1, HALO, w), lambda bi, c: (bi, jnp.maximum(c * hpc - 1, 0), cb))

    full = lambda shape: pl.BlockSpec(shape, lambda bi, c: (0,) * len(shape))
    return pl.pallas_call(
        functools.partial(_qk_prep_kernel, npad=npad),
        grid=(b, nc),
        in_specs=[main(qa_c), main(ka_c), main(qm_c), main(km_c), halo(qm_c), halo(km_c),
                  full((1, ATT_DH)), full((1, ATT_DH)), full((CONV_WIDTH, 2 * w)), full((1, 2 * w))],
        out_specs=[
            pl.BlockSpec((1, CHUNK, w), lambda bi, c: (bi, c, 0)),
            pl.BlockSpec((1, CHUNK, w), lambda bi, c: (bi, c, 0)),
            pl.BlockSpec((1, CHUNK, w), lambda bi, c: (bi, c, 0)),
            pl.BlockSpec((1, w, CHUNK), lambda bi, c: (bi, 0, c)),
        ],
        out_shape=[jax.ShapeDtypeStruct((b, lp, w), BF16)] * 3 + [jax.ShapeDtypeStruct((b, w, lp), BF16)],
        scratch_shapes=[pltpu.VMEM((HALO + CHUNK, w), F32)],
        compiler_params=_cparams(("parallel", "arbitrary")),
        name="qk_prep",
    )(proj3, proj3, proj3, proj3, proj3, proj3, gq, gk, conv_w, conv_b)


FOX_Q_SPLIT = 2


def _fox_kernel(q_ref, k_ref, v_ref, c_ref, o_ref, m_ref, acc_ref, s0_ref, s1_ref, *, tq, npad):
    i = pl.program_id(2)
    hq = tq // FOX_Q_SPLIT
    c0 = c_ref[0, i][:, 0:1]
    m_ref[...] = jnp.full_like(m_ref, NEG)
    acc_ref[...] = jnp.zeros_like(acc_ref)
    ones = jnp.ones((tq, LANE), BF16)

    def kv_start(j):
        return j * tq if isinstance(j, int) else pl.multiple_of(j * tq, tq)

    def scores(j, s_ref, masked):
        k = k_ref[0, pl.ds(kv_start(j), tq), :]
        s = lax.dot_general(q_ref[0], k, (((1,), (1,)), ((), ())), preferred_element_type=F32)
        s = s + (c0 - c_ref[0, j])
        if masked:
            qpos = i * tq + lax.broadcasted_iota(jnp.int32, (tq, tq), 0)
            kpos = j * tq + lax.broadcasted_iota(jnp.int32, (tq, tq), 1)
            s = jnp.where((kpos <= qpos) & (kpos >= npad), s, NEG)
        s_ref[...] = s

    def accumulate(j, s_ref):
        vext = jnp.concatenate([v_ref[0, pl.ds(kv_start(j), tq), :], ones], axis=1)
        for part in range(FOX_Q_SPLIT):
            rows = slice(part * hq, (part + 1) * hq)
            s = s_ref[rows, :]
            m_prev = m_ref[rows, :]
            m_new = jnp.maximum(m_prev, jnp.broadcast_to(jnp.max(s, axis=1, keepdims=True), (hq, LANE)))
            alpha = jnp.exp(m_prev - m_new)
            p = jnp.exp(s - jnp.concatenate([m_new] * (tq // LANE), axis=1))
            pv = jnp.dot(p.astype(BF16), vext, preferred_element_type=F32)
            acc_ref[rows, :] = jnp.concatenate([alpha, alpha], axis=1) * acc_ref[rows, :] + pv
            m_ref[rows, :] = m_new

    def step(j, s_cur, s_nxt, masked):
        scores(j + 1, s_nxt, masked)
        accumulate(j, s_cur)

    scores(0, s0_ref, True)

    def pair(p, carry):
        step(2 * p, s0_ref, s1_ref, False)
        step(2 * p + 1, s1_ref, s0_ref, False)
        return carry

    lax.fori_loop(0, lax.shift_right_arithmetic(i - 1, 1), pair, 0)

    @pl.when(i == 0)
    def _():
        accumulate(0, s0_ref)

    @pl.when((i > 0) & (i % 2 == 1))
    def _():
        step(i - 1, s0_ref, s1_ref, True)
        accumulate(i, s1_ref)

    @pl.when((i > 0) & (i % 2 == 0))
    def _():
        step(i - 2, s0_ref, s1_ref, False)
        step(i - 1, s1_ref, s0_ref, True)
        accumulate(i, s0_ref)

    o_ref[0] = (acc_ref[:, :ATT_DH] / acc_ref[:, ATT_DH:]).astype(o_ref.dtype)


def _fox_attention(qa, ka, proj3, v_col, c_rows, npad):
    b, lp, w = qa.shape
    tq = SEQ_TILE
    nq = lp // tq
    vcb = v_col // ATT_DH
    return pl.pallas_call(
        functools.partial(_fox_kernel, tq=tq, npad=npad),
        grid=(b, HEADS, nq),
        in_specs=[
            pl.BlockSpec((1, tq, ATT_DH), lambda bi, h, i: (bi, i, h)),
            pl.BlockSpec((1, lp, ATT_DH), lambda bi, h, i: (bi, 0, h)),
            pl.BlockSpec((1, lp, ATT_DH), lambda bi, h, i: (bi, 0, vcb + h)),
            pl.BlockSpec((1, nq, 1, tq), lambda bi, h, i: (bi * HEADS + h, 0, 0, 0)),
        ],
        out_specs=pl.BlockSpec((1, tq, ATT_DH), lambda bi, h, i: (bi, i, h)),
        out_shape=jax.ShapeDtypeStruct((b, lp, w), BF16),
        scratch_shapes=[pltpu.VMEM((tq, LANE), F32), pltpu.VMEM((tq, ATT_DH + LANE), F32),
                        pltpu.VMEM((tq, tq), F32), pltpu.VMEM((tq, tq), F32)],
        compiler_params=_cparams(("parallel", "parallel", "arbitrary")),
        name="fox_attention",
    )(qa, ka, proj3, c_rows)


def _mlstm_kernel(q_ref, kt_ref, v_ref, om_ref, col_ref, row_ref, hn_ref, o_ref, ct_ref, m_ref):
    c = pl.program_id(1)

    @pl.when(c == 0)
    def _():
        ct_ref[...] = jnp.zeros_like(ct_ref)
        m_ref[...] = jnp.zeros_like(m_ref)

    t = CHUNK
    r_i = lax.broadcasted_iota(jnp.int32, (t, t), 0)
    c_i = lax.broadcasted_iota(jnp.int32, (t, t), 1)
    causal = r_i >= c_i
    ones = jnp.ones((t, LANE), BF16)

    for hh in range(HEADS):
        q = q_ref[0, :, hh * ML_DK:(hh + 1) * ML_DK]
        kt = kt_ref[0, hh * ML_DK:(hh + 1) * ML_DK, :]
        v = v_ref[0, :, hh * ML_DV:(hh + 1) * ML_DV]
        vext = jnp.concatenate([v, ones], axis=1)
        bt_c = col_ref[0, :, HEADS + hh:HEADS + hh + 1]
        bt_r = row_ref[0, HEADS + hh:HEADS + hh + 1, :]
        it_r = row_ref[0, 2 * HEADS + hh:2 * HEADS + hh + 1, :]
        g = bt_r[:, t - 1:t]
        m = m_ref[hh:hh + 1, 0:1]

        dmat = jnp.where(causal, bt_c - bt_r + it_r, -jnp.inf)
        inter = bt_c + m
        m_row = jnp.maximum(inter, jnp.max(dmat, axis=1, keepdims=True))
        w_intra = jnp.exp(dmat - m_row)
        w_inter = jnp.exp(inter - m_row)
        qk = jnp.dot(q, kt, preferred_element_type=F32)
        a = (qk * w_intra).astype(BF16)
        ct = ct_ref[hh]
        nd = (jnp.dot(a, vext, preferred_element_type=F32)
              + w_inter * jnp.dot(q, ct.astype(BF16), preferred_element_type=F32))
        num = nd[:, :ML_DV]
        den = nd[:, ML_DV:ML_DV + 1]
        denom = jnp.maximum(jnp.abs(den), jnp.exp(-m_row))
        hv = num / denom

        src = g - bt_r + it_r
        m_new = jnp.maximum(g + m, jnp.max(src, axis=1, keepdims=True))
        w_src = jnp.exp(src - m_new)
        decay = jnp.exp(g + m - m_new)
        ks = (kt.astype(F32) * w_src).astype(BF16)
        ct_ref[hh] = decay * ct + jnp.dot(ks, vext, preferred_element_type=F32)
        m_ref[hh:hh + 1, :] = jnp.broadcast_to(m_new, (1, LANE))

        hv = hv * lax.rsqrt(jnp.mean(hv * hv, axis=-1, keepdims=True) + EPS)
        hv = hv * hn_ref[:, hh * ML_DV:(hh + 1) * ML_DV]
        og = _sigmoid(om_ref[0, :, hh * ML_DV:(hh + 1) * ML_DV].astype(F32))
        o_ref[0, :, hh * ML_DV:(hh + 1) * ML_DV] = (og * hv).astype(o_ref.dtype)


def _mlstm(qm, kmt, proj3, cols, gcol, grow, hnorm):
    b, lp, qw = qm.shape
    nc = lp // CHUNK
    vw = HEADS * ML_DV
    v_cb = cols["vm"] // vw
    om_cb = cols["om"] // vw
    return pl.pallas_call(
        _mlstm_kernel,
        grid=(b, nc),
        in_specs=[
            pl.BlockSpec((1, CHUNK, qw), lambda bi, c: (bi, c, 0)),
            pl.BlockSpec((1, qw, CHUNK), lambda bi, c: (bi, 0, c)),
            pl.BlockSpec((1, CHUNK, vw), lambda bi, c: (bi, c, v_cb)),
            pl.BlockSpec((1, CHUNK, vw), lambda bi, c: (bi, c, om_cb)),
            pl.BlockSpec((1, CHUNK, LANE), lambda bi, c: (bi, c, 0)),
            pl.BlockSpec((1, 3 * HEADS, CHUNK), lambda bi, c: (bi, 0, c)),
            pl.BlockSpec((1, vw), lambda bi, c: (0, 0)),
        ],
        out_specs=pl.BlockSpec((1, CHUNK, vw), lambda bi, c: (bi, c, 0)),
        out_shape=jax.ShapeDtypeStruct((b, lp, vw), BF16),
        scratch_shapes=[pltpu.VMEM((HEADS, ML_DK, ML_DV + LANE), F32), pltpu.VMEM((HEADS, LANE), F32)],
        compiler_params=_cparams(("parallel", "arbitrary")),
        name="mlstm",
    )(qm, kmt, proj3, proj3, gcol, grow, hnorm)


def _merge_kernel(ya_ref, ym_ref, wa_ref, wm_ref, ga_ref, gm_ref, o_ref):
    a = jnp.dot(ya_ref[...], wa_ref[...], preferred_element_type=F32)
    m = jnp.dot(ym_ref[...], wm_ref[...], preferred_element_type=F32)
    out = _sigmoid(ga_ref[...].astype(F32)) * a + _sigmoid(gm_ref[...].astype(F32)) * m
    o_ref[...] = out.astype(o_ref.dtype)


def _merge(y_att, y_ml, wa, wm, proj, cols):
    n, ka = y_att.shape
    km = y_ml.shape[1]
    d = wa.shape[1]
    tm = _divisor_tile(n, 1024, SUBLANE)
    tn = _divisor_tile(d, 1024, LANE)
    ga_cb = cols["ga"] // tn
    gm_cb = cols["gm"] // tn
    return pl.pallas_call(
        _merge_kernel,
        grid=(n // tm, d // tn),
        in_specs=[
            pl.BlockSpec((tm, ka), lambda i, j: (i, 0)),
            pl.BlockSpec((tm, km), lambda i, j: (i, 0)),
            pl.BlockSpec((ka, tn), lambda i, j: (0, j)),
            pl.BlockSpec((km, tn), lambda i, j: (0, j)),
            pl.BlockSpec((tm, tn), lambda i, j: (i, ga_cb + j)),
            pl.BlockSpec((tm, tn), lambda i, j: (i, gm_cb + j)),
        ],
        out_specs=pl.BlockSpec((tm, tn), lambda i, j: (i, j)),
        out_shape=jax.ShapeDtypeStruct((n, d), BF16),
        compiler_params=_cparams(("parallel", "arbitrary")),
        name="merge",
    )(y_att, y_ml, wa, wm, proj, proj)


def _out_residual_kernel(x_ref, w_ref, h_ref, o_ref, *, tm, lp, npad, nbatch):
    mix = jnp.dot(x_ref[...], w_ref[...], preferred_element_type=F32)
    valid = _row_valid(pl.program_id(0) * tm, tm, lp, npad, nbatch)
    o_ref[...] = h_ref[...] + jnp.where(valid, mix, 0.0)


def _out_residual(x, w, h, lp, npad, nbatch):
    n, k = x.shape
    d = w.shape[1]
    tm = _divisor_tile(n, 1024, SUBLANE)
    tn = _divisor_tile(d, 1024, LANE)
    return pl.pallas_call(
        functools.partial(_out_residual_kernel, tm=tm, lp=lp, npad=npad, nbatch=nbatch),
        grid=(n // tm, d // tn),
        in_specs=[
            pl.BlockSpec((tm, k), lambda i, j: (i, 0)),
            pl.BlockSpec((k, tn), lambda i, j: (0, j)),
            pl.BlockSpec((tm, tn), lambda i, j: (i, j)),
        ],
        out_specs=pl.BlockSpec((tm, tn), lambda i, j: (i, j)),
        out_shape=jax.ShapeDtypeStruct((n, d), F32),
        compiler_params=_cparams(("parallel", "arbitrary")),
        name="out_residual",
    )(x, w, h)


def _ffn_kernel(x_ref, g_ref, wg_ref, wu_ref, wd_ref, o_ref, xn_ref, *, tm, lp, npad, nbatch):
    f = pl.program_id(1)

    @pl.when(f == 0)
    def _():
        x = x_ref[...]
        ms = jnp.mean(x * x, axis=-1, keepdims=True)
        xn_ref[...] = (x * lax.rsqrt(ms + EPS) * g_ref[...]).astype(BF16)
        o_ref[...] = jnp.zeros_like(o_ref)

    xn = xn_ref[...]
    gate = jnp.dot(xn, wg_ref[...], preferred_element_type=F32)
    up = jnp.dot(xn, wu_ref[...], preferred_element_type=F32)
    act = (gate * _sigmoid(gate) * up).astype(BF16)
    o_ref[...] += jnp.dot(act, wd_ref[...], preferred_element_type=F32)

    @pl.when(f == pl.num_programs(1) - 1)
    def _():
        valid = _row_valid(pl.program_id(0) * tm, tm, lp, npad, nbatch)
        o_ref[...] = x_ref[...] + jnp.where(valid, o_ref[...], 0.0)


def _ffn(h, gain, wg, wu, wd, lp, npad, nbatch):
    n, d = h.shape
    dff = wg.shape[1]
    tm = _divisor_tile(n, 1024, SUBLANE)
    tf = _divisor_tile(dff, 512, LANE)
    return pl.pallas_call(
        functools.partial(_ffn_kernel, tm=tm, lp=lp, npad=npad, nbatch=nbatch),
        grid=(n // tm, dff // tf),
        in_specs=[
            pl.BlockSpec((tm, d), lambda i, f: (i, 0)),
            pl.BlockSpec((1, d), lambda i, f: (0, 0)),
            pl.BlockSpec((d, tf), lambda i, f: (0, f)),
            pl.BlockSpec((d, tf), lambda i, f: (0, f)),
            pl.BlockSpec((tf, d), lambda i, f: (f, 0)),
        ],
        out_specs=pl.BlockSpec((tm, d), lambda i, f: (i, 0)),
        out_shape=jax.ShapeDtypeStruct((n, d), F32),
        scratch_shapes=[pltpu.VMEM((tm, d), BF16)],
        compiler_params=_cparams(("parallel", "arbitrary")),
        name="ffn",
    )(h, gain, wg, wu, wd)


def _bf16_bits(x):
    bits = lax.bitcast_convert_type(x, jnp.uint32)
    rounded = bits + jnp.uint32(0x7FFF) + ((bits >> 16) & jnp.uint32(1))
    return rounded >> 16


def _router_kernel(x_ref, g_ref, rw_ref, xp_ref, route_ref):
    x = x_ref[...]
    d = x.shape[1]
    ms = jnp.mean(x * x, axis=-1, keepdims=True)
    xn = x * lax.rsqrt(ms + EPS) * g_ref[...]
    xp_ref[...] = _bf16_bits(xn[:, :d // 2]) | (_bf16_bits(xn[:, d // 2:]) << 16)

    logits = jnp.dot(xn, rw_ref[...], preferred_element_type=F32, precision=HIGHEST)
    lane = lax.broadcasted_iota(jnp.int32, logits.shape, 1)
    logits = jnp.where(lane < N_EXPERTS, logits, -jnp.inf)
    v1 = jnp.max(logits, axis=-1, keepdims=True)
    i1 = jnp.min(jnp.where(logits == v1, lane, LANE), axis=-1, keepdims=True)
    rest = jnp.where(lane == i1, -jnp.inf, logits)
    v2 = jnp.max(rest, axis=-1, keepdims=True)
    i2 = jnp.min(jnp.where(rest == v2, lane, LANE), axis=-1, keepdims=True)
    e = jnp.exp(v2 - v1)
    g1 = 1.0 / (1.0 + e)
    g2 = e / (1.0 + e)
    route = jnp.where(lane == 0, i1.astype(F32),
                      jnp.where(lane == 1, i2.astype(F32),
                                jnp.where(lane == 2, g1, jnp.where(lane == 3, g2, 0.0))))
    route_ref[...] = route


def _router(h, gain, rw):
    n, d = h.shape
    tm = _divisor_tile(n, 1024, SUBLANE)
    return pl.pallas_call(
        _router_kernel,
        grid=(n // tm,),
        in_specs=[
            pl.BlockSpec((tm, d), lambda i: (i, 0)),
            pl.BlockSpec((1, d), lambda i: (0, 0)),
            pl.BlockSpec((d, LANE), lambda i: (0, 0)),
        ],
        out_specs=[
            pl.BlockSpec((tm, d // 2), lambda i: (i, 0)),
            pl.BlockSpec((tm, LANE), lambda i: (i, 0)),
        ],
        out_shape=[jax.ShapeDtypeStruct((n, d // 2), jnp.uint32), jax.ShapeDtypeStruct((n, LANE), F32)],
        compiler_params=_cparams(("parallel",)),
        name="router",
    )(h, gain, rw)


GATHER_ROWS = 512


def _row_copy(src_hbm, src_row, dst_ref, dst_row, sem):
    return pltpu.make_async_copy(src_hbm.at[pl.ds(src_row, 1)], dst_ref.at[pl.ds(dst_row, 1)], sem)


def _gather_kernel(idx_ref, x_hbm, o_ref, sem, *, rows):
    def start(r2, carry):
        for u in range(2):
            r = 2 * r2 + u
            _row_copy(x_hbm, idx_ref[0, 0, r], o_ref, r, sem).start(priority=u)
        return carry

    lax.fori_loop(0, rows // 2, start, 0)

    def wait(r, carry):
        _row_copy(x_hbm, 0, o_ref, r, sem).wait()
        return carry

    lax.fori_loop(0, rows, wait, 0)


def _moe_gather(xp, slot_tok, rows):
    p = slot_tok.shape[0]
    w = xp.shape[1]
    nb = p // rows
    idx3 = slot_tok.reshape(nb, 1, rows)
    return pl.pallas_call(
        functools.partial(_gather_kernel, rows=rows),
        grid=(nb,),
        in_specs=[
            pl.BlockSpec((1, 1, rows), lambda i: (i, 0, 0), memory_space=pltpu.SMEM),
            pl.BlockSpec(memory_space=pl.ANY),
        ],
        out_specs=pl.BlockSpec((rows, w), lambda i: (i, 0)),
        out_shape=jax.ShapeDtypeStruct((p, w), xp.dtype),
        scratch_shapes=[pltpu.SemaphoreType.DMA(())],
        compiler_params=_cparams(("arbitrary",)),
        name="moe_gather",
    )(idx3, xp)


def _moe_ffn_kernel(blk_e_ref, nblk_ref, xg_ref, wg_ref, wu_ref, wd_ref, o_ref, xn_ref):
    i = pl.program_id(0)
    f = pl.program_id(1)

    @pl.when(i < nblk_ref[0])
    def _():
        @pl.when(f == 0)
        def _():
            packed = xg_ref[...]
            half = packed.shape[1]
            lo = lax.bitcast_convert_type(packed << 16, F32)
            hi = lax.bitcast_convert_type(packed & jnp.uint32(0xFFFF0000), F32)
            xn_ref[:, :half] = lo.astype(BF16)
            xn_ref[:, half:] = hi.astype(BF16)
            o_ref[...] = jnp.zeros_like(o_ref)

        xn = xn_ref[...]
        gate = jnp.dot(xn, wg_ref[0], preferred_element_type=F32)
        up = jnp.dot(xn, wu_ref[0], preferred_element_type=F32)
        act = (gate * _sigmoid(gate) * up).astype(BF16)
        o_ref[...] += jnp.dot(act, wd_ref[0], preferred_element_type=F32)

    @pl.when((i >= nblk_ref[0]) & (f == 0))
    def _():
        o_ref[...] = jnp.zeros_like(o_ref)


def _moe_ffn(xg, blk_e, nblk, wg, wu, wd, tm):
    p, half = xg.shape
    d = 2 * half
    dff = wg.shape[2]
    tf = _divisor_tile(dff, 512, LANE)
    nf = dff // tf
    nb = p // tm

    def row_map(i, f, be, nu):
        return (jnp.minimum(i, nu[0] - 1), 0)

    def f_eff(i, f, nu):
        return jnp.where(i < nu[0], f, nf - 1)

    grid_spec = pltpu.PrefetchScalarGridSpec(
        num_scalar_prefetch=2,
        grid=(nb, nf),
        in_specs=[
            pl.BlockSpec((tm, half), row_map),
            pl.BlockSpec((1, d, tf), lambda i, f, be, nu: (be[i], 0, f_eff(i, f, nu))),
            pl.BlockSpec((1, d, tf), lambda i, f, be, nu: (be[i], 0, f_eff(i, f, nu))),
            pl.BlockSpec((1, tf, d), lambda i, f, be, nu: (be[i], f_eff(i, f, nu), 0)),
        ],
        out_specs=pl.BlockSpec((tm, d), lambda i, f, be, nu: (i, 0)),
        scratch_shapes=[pltpu.VMEM((tm, d), BF16)],
    )
    return pl.pallas_call(
        _moe_ffn_kernel,
        grid_spec=grid_spec,
        out_shape=jax.ShapeDtypeStruct((p, d), F32),
        compiler_params=_cparams(("arbitrary", "arbitrary")),
        name="moe_ffn",
    )(blk_e, nblk, xg, wg, wu, wd)


def _combine_kernel(d_ref, route_ref, h_ref, y_hbm, o_ref, y0_ref, y1_ref, sem, *, rows, lp, npad, nbatch):
    def start(r, carry):
        _row_copy(y_hbm, d_ref[0, 0, 2 * r], y0_ref, r, sem).start(priority=0)
        _row_copy(y_hbm, d_ref[0, 0, 2 * r + 1], y1_ref, r, sem).start(priority=1)
        return carry

    lax.fori_loop(0, rows, start, 0)

    def wait(r, carry):
        _row_copy(y_hbm, 0, y0_ref, r, sem).wait()
        _row_copy(y_hbm, 0, y1_ref, r, sem).wait()
        return carry

    lax.fori_loop(0, rows, wait, 0)

    g0 = route_ref[:, 2:3]
    g1 = route_ref[:, 3:4]
    y = g0 * y0_ref[...] + g1 * y1_ref[...]
    valid = _row_valid(pl.program_id(0) * rows, rows, lp, npad, nbatch)
    o_ref[...] = h_ref[...] + jnp.where(valid, y, 0.0)


def _moe_combine(h, route, dest, yg, rows, lp, npad, nbatch):
    n, d = h.shape
    nb = n // rows
    d3 = dest.reshape(nb, 1, 2 * rows)
    return pl.pallas_call(
        functools.partial(_combine_kernel, rows=rows, lp=lp, npad=npad, nbatch=nbatch),
        grid=(nb,),
        in_specs=[
            pl.BlockSpec((1, 1, 2 * rows), lambda i: (i, 0, 0), memory_space=pltpu.SMEM),
            pl.BlockSpec((rows, LANE), lambda i: (i, 0)),
            pl.BlockSpec((rows, d), lambda i: (i, 0)),
            pl.BlockSpec(memory_space=pl.ANY),
        ],
        out_specs=pl.BlockSpec((rows, d), lambda i: (i, 0)),
        out_shape=jax.ShapeDtypeStruct((n, d), F32),
        scratch_shapes=[pltpu.VMEM((rows, d), F32), pltpu.VMEM((rows, d), F32), pltpu.SemaphoreType.DMA(())],
        compiler_params=_cparams(("arbitrary",)),
        name="moe_combine",
    )(d3, route, h, yg)


def _moe_layout(experts, tm, nblocks):
    n = experts.shape[0]
    flat_e = experts.reshape(-1)
    onehot = (flat_e[:, None] == jnp.arange(N_EXPERTS, dtype=jnp.int32)[None, :]).astype(jnp.int32)
    csum = jnp.cumsum(onehot, axis=0)
    pos = jnp.take_along_axis(csum, flat_e[:, None], axis=1)[:, 0] - 1
    counts = csum[-1]
    padded = (counts + tm - 1) // tm * tm
    ends = jnp.cumsum(padded)
    gstart = ends - padded
    dest = (gstart[flat_e] + pos).astype(jnp.int32)
    flat_tok = jnp.arange(2 * n, dtype=jnp.int32) // 2
    slot_tok = jnp.zeros((nblocks * tm,), jnp.int32).at[dest].set(flat_tok)
    nblk = (ends[-1] // tm).astype(jnp.int32)
    blk = jnp.minimum(jnp.arange(nblocks, dtype=jnp.int32), nblk - 1)
    blk_e = jnp.minimum(jnp.searchsorted(ends, blk * tm, side="right"), N_EXPERTS - 1).astype(jnp.int32)
    return dest, slot_tok, blk_e, nblk.reshape(1)


def _moe_block(h, gain, rw, wg, wu, wd, lp, npad, nbatch):
    n, d = h.shape
    xp, route = _router(h, gain, rw)
    experts = route[:, :2].astype(jnp.int32)
    tm = _divisor_tile(n, 1024, GATHER_ROWS) if n % GATHER_ROWS == 0 else n
    rows = min(GATHER_ROWS, tm)
    nblocks = -(-(2 * n + N_EXPERTS * (tm - 1)) // tm)
    dest, slot_tok, blk_e, nblk = _moe_layout(experts, tm, nblocks)
    xg = _moe_gather(xp, slot_tok, rows)
    yg = _moe_ffn(xg, blk_e, nblk, wg, wu, wd, tm)
    return _moe_combine(h, route, dest, yg, rows, lp, npad, nbatch)


def _inproj_layout(d_model):
    att_w = HEADS * ATT_DH
    qk_w = HEADS * ML_DK
    v_w = HEADS * ML_DV
    names = ("qa", "ka", "va", "fa", "qm", "km", "vm", "im", "fm", "om", "ga", "gm")
    widths = (att_w, att_w, att_w, HEADS, qk_w, qk_w, v_w, HEADS, HEADS, v_w, d_model, d_model)
    src, off = {}, 0
    for nm, wd in zip(names, widths):
        src[nm] = (off, wd)
        off += wd
    big_order = ("vm", "om", "ga", "gm", "qa", "ka", "va", "qm", "km")
    cols, off = {}, 0
    for nm in big_order:
        cols[nm] = off
        off += src[nm][1]
    return src, big_order, cols


def kernel(x, meta_tokens, ln_mix, w_in, att_q_norm, att_k_norm, att_f_bias, ml_conv_w, ml_conv_b, ml_i_bias,
           ml_f_bias, ml_h_norm, w_branch_att, w_branch_ml, w_out, ln_ffn, dense_w_gate, dense_w_up, dense_w_down,
           router_w, moe_w_gate, moe_w_up, moe_w_down):
    nbatch, seq, d = x.shape
    depth = w_in.shape[0]
    lp = -(-(CHUNK + seq) // SEQ_TILE) * SEQ_TILE
    npad = lp - seq - N_META
    n = nbatch * lp
    src, big_order, cols = _inproj_layout(d)

    h = jnp.concatenate([
        jnp.zeros((nbatch, npad, d), x.dtype),
        jnp.broadcast_to(meta_tokens[None].astype(x.dtype), (nbatch, N_META, d)),
        x], axis=1).reshape(n, d)

    for layer in range(depth):
        wl = w_in[layer]
        w_big = jnp.concatenate([wl[:, src[nm][0]:src[nm][0] + src[nm][1]] for nm in big_order], axis=1).astype(BF16)
        w_gates = jnp.concatenate(
            [wl[:, src[nm][0]:src[nm][0] + HEADS] for nm in ("fa", "fm", "im")]
            + [jnp.zeros((d, LANE - 3 * HEADS), F32)], axis=1)
        gate_bias = jnp.concatenate([att_f_bias[layer], ml_f_bias[layer], ml_i_bias[layer],
                                     jnp.zeros((LANE - 3 * HEADS,), F32)]).reshape(1, LANE)

        proj, gates = _norm_inproj(h, ln_mix[layer].reshape(1, d), w_big, w_gates)
        proj3 = proj.reshape(nbatch, lp, -1)
        gcol, grow = _gate_prep(gates.reshape(nbatch, lp, LANE), gate_bias, npad)
        qa, ka, qm, kmt = _qk_prep(proj3, cols, att_q_norm[layer].reshape(1, ATT_DH),
                                   att_k_norm[layer].reshape(1, ATT_DH), ml_conv_w[layer],
                                   ml_conv_b[layer].reshape(1, -1), npad)
        c_rows = grow[:, :HEADS, :].reshape(nbatch * HEADS, lp // SEQ_TILE, 1, SEQ_TILE)
        y_att = _fox_attention(qa, ka, proj3, cols["va"], c_rows, npad)
        y_ml = _mlstm(qm, kmt, proj3, cols, gcol, grow, ml_h_norm[layer].reshape(1, -1))
        merged = _merge(y_att.reshape(n, -1), y_ml.reshape(n, -1), w_branch_att[layer].astype(BF16),
                        w_branch_ml[layer].astype(BF16), proj, cols)
        h = _out_residual(merged, w_out[layer].astype(BF16), h, lp, npad, nbatch)

        j = layer // 2
        gain = ln_ffn[layer].reshape(1, d)
        if layer % 2 == 0:
            h = _ffn(h, gain, dense_w_gate[j].astype(BF16), dense_w_up[j].astype(BF16),
                     dense_w_down[j].astype(BF16), lp, npad, nbatch)
        else:
            rw = jnp.concatenate([router_w[j], jnp.zeros((d, LANE - N_EXPERTS), F32)], axis=1)
            h = _moe_block(h, gain, rw, moe_w_gate[j].astype(BF16), moe_w_up[j].astype(BF16),
                           moe_w_down[j].astype(BF16), lp, npad, nbatch)

    return h.reshape(nbatch, lp, d)[:, lp - seq:, :]
```

```python
import functools

import jax
import jax.numpy as jnp
from jax import lax
from jax.experimental import pallas as pl
from jax.experimental.pallas import tpu as pltpu

N_META = 16
CHUNK = 128
HEADS = 8
ATT_DH = 128
ML_DK = 128
ML_DV = 256
CONV_WIDTH = 4
N_EXPERTS = 8
EPS = 1e-6
NEG = -1e30

LANE = 128
SUBLANE = 8
VMEM_BYTES_V7X = 64 * 1024 * 1024
VMEM_LIMIT = 56 * 1024 * 1024

SEQ_TILE = 512
HALO = SUBLANE

F32 = jnp.float32
BF16 = jnp.bfloat16
HIGHEST = lax.Precision.HIGHEST


def _divisor_tile(n, pref, align):
    if n <= pref:
        return n
    t = (pref // align) * align
    while t >= align:
        if n % t == 0:
            return t
        t -= align
    return n


def _cparams(sem):
    return pltpu.CompilerParams(dimension_semantics=sem, vmem_limit_bytes=VMEM_LIMIT)


def _row_valid(row0, nrows, lp, npad, nbatch):
    r = row0 + lax.broadcasted_iota(jnp.int32, (nrows, 1), 0)
    invalid = (r >= 0) & (r < npad)
    for b in range(1, nbatch):
        invalid = invalid | ((r >= b * lp) & (r < b * lp + npad))
    return jnp.logical_not(invalid)


def _log_sigmoid(x):
    return jnp.minimum(x, 0.0) - jnp.log1p(jnp.exp(-jnp.abs(x)))


def _sigmoid(x):
    return 1.0 / (1.0 + jnp.exp(-x))


def _dot_f32_3pass(x, w):
    xh = x.astype(BF16)
    xl = (x - xh.astype(F32)).astype(BF16)
    wh = w.astype(BF16)
    wl = (w - wh.astype(F32)).astype(BF16)
    return (jnp.dot(xh, wh, preferred_element_type=F32) + jnp.dot(xl, wh, preferred_element_type=F32)
            + jnp.dot(xh, wl, preferred_element_type=F32))


def _norm_inproj_kernel(x_ref, g_ref, w_ref, wg_ref, o_ref, og_ref, xn_ref):
    @pl.when(pl.program_id(1) == 0)
    def _():
        x = x_ref[...]
        ms = jnp.mean(x * x, axis=-1, keepdims=True)
        xn = x * lax.rsqrt(ms + EPS) * g_ref[...]
        xn_ref[...] = xn.astype(BF16)
        og_ref[...] = _dot_f32_3pass(xn, wg_ref[...])

    o_ref[...] = jnp.dot(xn_ref[...], w_ref[...], preferred_element_type=F32).astype(o_ref.dtype)


def _norm_inproj(h, gain, w_big, w_gates):
    n, d = h.shape
    wn = w_big.shape[1]
    tm = _divisor_tile(n, 1024, SUBLANE)
    tn = _divisor_tile(wn, 1024, LANE)
    return pl.pallas_call(
        _norm_inproj_kernel,
        grid=(n // tm, wn // tn),
        in_specs=[
            pl.BlockSpec((tm, d), lambda i, j: (i, 0)),
            pl.BlockSpec((1, d), lambda i, j: (0, 0)),
            pl.BlockSpec((d, tn), lambda i, j: (0, j)),
            pl.BlockSpec((d, LANE), lambda i, j: (0, 0)),
        ],
        out_specs=[
            pl.BlockSpec((tm, tn), lambda i, j: (i, j)),
            pl.BlockSpec((tm, LANE), lambda i, j: (i, 0)),
        ],
        out_shape=[jax.ShapeDtypeStruct((n, wn), BF16), jax.ShapeDtypeStruct((n, LANE), F32)],
        scratch_shapes=[pltpu.VMEM((tm, d), BF16)],
        compiler_params=_cparams(("parallel", "arbitrary")),
        name="norm_inproj",
    )(h, gain, w_big, w_gates)


def _gate_prep_kernel(g_ref, b_ref, col_ref, row_ref, carry_ref, *, npad):
    c = pl.program_id(1)

    @pl.when(c == 0)
    def _():
        carry_ref[...] = jnp.zeros_like(carry_ref)

    x = g_ref[0] + b_ref[...]
    lane = lax.broadcasted_iota(jnp.int32, (CHUNK, LANE), 1)
    pos = c * CHUNK + lax.broadcasted_iota(jnp.int32, (CHUNK, LANE), 0)
    valid = pos >= npad
    ls = _log_sigmoid(x)
    z = jnp.where(lane < HEADS, ls, jnp.where((lane < 2 * HEADS) & valid, ls, 0.0))
    r_i = lax.broadcasted_iota(jnp.int32, (CHUNK, CHUNK), 0)
    c_i = lax.broadcasted_iota(jnp.int32, (CHUNK, CHUNK), 1)
    tril = (r_i >= c_i).astype(F32)
    s = jnp.dot(tril, z, preferred_element_type=F32, precision=HIGHEST)
    s = s + jnp.where(lane < HEADS, carry_ref[0:1, :], 0.0)
    carry_ref[...] = jnp.broadcast_to(s[CHUNK - 1:CHUNK, :], carry_ref.shape)
    ipre = jnp.where(valid, x, -jnp.inf)
    col = jnp.where(lane < 2 * HEADS, s, jnp.where(lane < 3 * HEADS, ipre, 0.0))
    col_ref[0] = col
    row_ref[0] = col.T[:3 * HEADS, :]


def _gate_prep(gates3, bias, npad):
    b, lp, _ = gates3.shape
    nc = lp // CHUNK
    return pl.pallas_call(
        functools.partial(_gate_prep_kernel, npad=npad),
        grid=(b, nc),
        in_specs=[
            pl.BlockSpec((1, CHUNK, LANE), lambda bi, c: (bi, c, 0)),
            pl.BlockSpec((1, LANE), lambda bi, c: (0, 0)),
        ],
        out_specs=[
            pl.BlockSpec((1, CHUNK, LANE), lambda bi, c: (bi, c, 0)),
            pl.BlockSpec((1, 3 * HEADS, CHUNK), lambda bi, c: (bi, 0, c)),
        ],
        out_shape=[jax.ShapeDtypeStruct((b, lp, LANE), F32), jax.ShapeDtypeStruct((b, 3 * HEADS, lp), F32)],
        scratch_shapes=[pltpu.VMEM((SUBLANE, LANE), F32)],
        compiler_params=_cparams(("parallel", "arbitrary")),
        name="gate_prep",
    )(gates3, bias)


def _qk_prep_kernel(qa_ref, ka_ref, qm_ref, km_ref, qmh_ref, kmh_ref, gq_ref, gk_ref, cw_ref, cb_ref,
                    qa_o, ka_o, qm_o, kmt_o, win_ref, *, npad):
    c = pl.program_id(1)
    ml_w = HEADS * ML_DK

    for hh in range(HEADS):
        sl = slice(hh * ATT_DH, (hh + 1) * ATT_DH)
        q = qa_ref[0, :, sl].astype(F32)
        q = q * lax.rsqrt(jnp.mean(q * q, axis=-1, keepdims=True) + EPS) * gq_ref[...]
        qa_o[0, :, sl] = (q * (ATT_DH ** -0.5)).astype(BF16)
        k = ka_ref[0, :, sl].astype(F32)
        k = k * lax.rsqrt(jnp.mean(k * k, axis=-1, keepdims=True) + EPS) * gk_ref[...]
        ka_o[0, :, sl] = k.astype(BF16)

    pos = c * CHUNK + lax.broadcasted_iota(jnp.int32, (CHUNK, 1), 0)
    hpos = c * CHUNK - HALO + lax.broadcasted_iota(jnp.int32, (HALO, 1), 0)
    row_ok = pos >= npad
    halo_ok = (hpos >= npad) & (c > 0)

    def conv_silu(x_ref, xh_ref, col0):
        win_ref[0:HALO, :] = jnp.where(halo_ok, xh_ref[0].astype(F32), 0.0)
        win_ref[HALO:HALO + CHUNK, :] = jnp.where(row_ok, x_ref[0].astype(F32), 0.0)
        y = jnp.broadcast_to(cb_ref[:, col0:col0 + ml_w], (CHUNK, ml_w))
        for j in range(CONV_WIDTH):
            off = HALO - (CONV_WIDTH - 1) + j
            y = y + cw_ref[j:j + 1, col0:col0 + ml_w] * win_ref[off:off + CHUNK, :]
        return y * _sigmoid(y)

    yq = conv_silu(qm_ref, qmh_ref, 0)
    qm_o[0] = yq.astype(BF16)
    yk = conv_silu(km_ref, kmh_ref, ml_w) * (ML_DK ** -0.5)
    for hh in range(HEADS):
        sl = slice(hh * ML_DK, (hh + 1) * ML_DK)
        kmt_o[0, sl, :] = yk[:, sl].T.astype(BF16)


def _qk_prep(proj3, cols, gq, gk, conv_w, conv_b, npad):
    b, lp, _ = proj3.shape
    nc = lp // CHUNK
    w = HEADS * ATT_DH
    qa_c, ka_c, qm_c, km_c = (cols[k] // w for k in ("qa", "ka", "qm", "km"))
    hpc = CHUNK // HALO

    def main(cb):
        return pl.BlockSpec((1, CHUNK, w), lambda bi, c: (bi, c, cb))

    def halo(cb):
        return pl.BlockSpec((1, HALO, w), lambda bi, c: (bi, jnp.maximum(c * hpc - 1, 0), cb))

    full = lambda shape: pl.BlockSpec(shape, lambda bi, c: (0,) * len(shape))
    return pl.pallas_call(
        functools.partial(_qk_prep_kernel, npad=npad),
        grid=(b, nc),
        in_specs=[main(qa_c), main(ka_c), main(qm_c), main(km_c), halo(qm_c), halo(km_c),
                  full((1, ATT_DH)), full((1, ATT_DH)), full((CONV_WIDTH, 2 * w)), full((1, 2 * w))],
        out_specs=[
            pl.BlockSpec((1, CHUNK, w), lambda bi, c: (bi, c, 0)),
            pl.BlockSpec((1, CHUNK, w), lambda bi, c: (bi, c, 0)),
            pl.BlockSpec((1, CHUNK, w), lambda bi, c: (bi, c, 0)),
            pl.BlockSpec((1, w, CHUNK), lambda bi, c: (bi, 0, c)),
        ],
        out_shape=[jax.ShapeDtypeStruct((b, lp, w), BF16)] * 3 + [jax.ShapeDtypeStruct((b, w, lp), BF16)],
        scratch_shapes=[pltpu.VMEM((HALO + CHUNK, w), F32)],
        compiler_params=_cparams(("parallel", "arbitrary")),
        name="qk_prep",
    )(proj3, proj3, proj3, proj3, proj3, proj3, gq, gk, conv_w, conv_b)


FOX_PART_ROWS = 256
FOX_Q_TILE_MAX = 1536


def _fox_kernel(q_ref, k_ref, v_ref, c_ref, o_ref, m_ref, acc_ref, s0_ref, s1_ref, *, tq, tk, npad):
    i = pl.program_id(2)
    r = tq // tk
    hq = min(FOX_PART_ROWS, tk)
    slots = (s0_ref, s1_ref)
    c0 = c_ref[0, r * i][:, 0:1]
    m_ref[...] = jnp.full_like(m_ref, NEG)
    acc_ref[...] = jnp.zeros_like(acc_ref)
    ones = jnp.ones((tk, LANE), BF16)

    def kv_start(j):
        return j * tk if isinstance(j, int) else pl.multiple_of(j * tk, tk)

    def scores(j, s_ref, masked, row0):
        k = k_ref[0, pl.ds(kv_start(j), tk), :]
        s = lax.dot_general(q_ref[0, row0:, :], k, (((1,), (1,)), ((), ())), preferred_element_type=F32)
        s = s + (c0 - c_ref[0, j])
        if masked:
            qpos = i * tq + row0 + lax.broadcasted_iota(jnp.int32, (tq - row0, tk), 0)
            kpos = j * tk + lax.broadcasted_iota(jnp.int32, (tq - row0, tk), 1)
            s = jnp.where((kpos <= qpos) & (kpos >= npad), s, NEG)
        s_ref[row0:, :] = s

    def accumulate(j, s_ref, row0):
        vext = jnp.concatenate([v_ref[0, pl.ds(kv_start(j), tk), :], ones], axis=1)
        for part in range(row0 // hq, tq // hq):
            rows = slice(part * hq, (part + 1) * hq)
            s = s_ref[rows, :]
            m_prev = m_ref[rows, :]
            m_new = jnp.maximum(m_prev, jnp.broadcast_to(jnp.max(s, axis=1, keepdims=True), (hq, LANE)))
            alpha = jnp.exp(m_prev - m_new)
            p = jnp.exp(s - jnp.concatenate([m_new] * (tk // LANE), axis=1))
            pv = jnp.dot(p.astype(BF16), vext, preferred_element_type=F32)
            acc_ref[rows, :] = jnp.concatenate([alpha, alpha], axis=1) * acc_ref[rows, :] + pv
            m_ref[rows, :] = m_new

    def run(chunks):
        for n, (j, d) in enumerate(chunks):
            if n + 1 < len(chunks):
                jn, dn = chunks[n + 1]
                scores(jn, slots[(n + 1) % 2], dn is not None, 0 if dn is None else dn * tk)
            accumulate(j, slots[n % 2], 0 if d is None else d * tk)

    first = r * i
    diag = [(first + d, d) for d in range(r)]
    scores(0, s0_ref, True, 0)

    n_plain = jnp.maximum(first - 1, 0)
    odd = lax.bitwise_and(n_plain, 1)

    def pair_body(p, carry):
        j = 2 * p
        scores(j + 1, s1_ref, False, 0)
        accumulate(j, s0_ref, 0)
        scores(j + 2, s0_ref, False, 0)
        accumulate(j + 1, s1_ref, 0)
        return carry

    lax.fori_loop(0, lax.shift_right_logical(n_plain, 1), pair_body, 0)

    @pl.when(i == 0)
    def _():
        run(diag)

    @pl.when((i > 0) & (odd == 0))
    def _():
        run([(first - 1, None)] + diag)

    @pl.when((i > 0) & (odd == 1))
    def _():
        run([(first - 2, None), (first - 1, None)] + diag)

    o_ref[0] = (acc_ref[:, :ATT_DH] / acc_ref[:, ATT_DH:]).astype(o_ref.dtype)


def _fox_attention(qa, ka, proj3, v_col, c_rows, npad):
    b, lp, w = qa.shape
    tk = SEQ_TILE
    tq = _divisor_tile(lp, FOX_Q_TILE_MAX, tk)
    nq = lp // tq
    nk = lp // tk
    vcb = v_col // ATT_DH
    return pl.pallas_call(
        functools.partial(_fox_kernel, tq=tq, tk=tk, npad=npad),
        grid=(b, HEADS, nq),
        in_specs=[
            pl.BlockSpec((1, tq, ATT_DH), lambda bi, h, i: (bi, i, h)),
            pl.BlockSpec((1, lp, ATT_DH), lambda bi, h, i: (bi, 0, h)),
            pl.BlockSpec((1, lp, ATT_DH), lambda bi, h, i: (bi, 0, vcb + h)),
            pl.BlockSpec((1, nk, 1, tk), lambda bi, h, i: (bi * HEADS + h, 0, 0, 0)),
        ],
        out_specs=pl.BlockSpec((1, tq, ATT_DH), lambda bi, h, i: (bi, i, h)),
        out_shape=jax.ShapeDtypeStruct((b, lp, w), BF16),
        scratch_shapes=[pltpu.VMEM((tq, LANE), F32), pltpu.VMEM((tq, ATT_DH + LANE), F32),
                        pltpu.VMEM((tq, tk), F32), pltpu.VMEM((tq, tk), F32)],
        compiler_params=_cparams(("parallel", "parallel", "arbitrary")),
        name="fox_attention",
    )(qa, ka, proj3, c_rows)


def _mlstm_kernel(q_ref, kt_ref, v_ref, om_ref, col_ref, row_ref, hn_ref, o_ref, ct_ref, m_ref):
    c = pl.program_id(1)

    @pl.when(c == 0)
    def _():
        ct_ref[...] = jnp.zeros_like(ct_ref)
        m_ref[...] = jnp.zeros_like(m_ref)

    t = CHUNK
    r_i = lax.broadcasted_iota(jnp.int32, (t, t), 0)
    c_i = lax.broadcasted_iota(jnp.int32, (t, t), 1)
    causal = r_i >= c_i
    ones = jnp.ones((t, LANE), BF16)

    for hh in range(HEADS):
        q = q_ref[0, :, hh * ML_DK:(hh + 1) * ML_DK]
        kt = kt_ref[0, hh * ML_DK:(hh + 1) * ML_DK, :]
        v = v_ref[0, :, hh * ML_DV:(hh + 1) * ML_DV]
        vext = jnp.concatenate([v, ones], axis=1)
        bt_c = col_ref[0, :, HEADS + hh:HEADS + hh + 1]
        bt_r = row_ref[0, HEADS + hh:HEADS + hh + 1, :]
        it_r = row_ref[0, 2 * HEADS + hh:2 * HEADS + hh + 1, :]
        g = bt_r[:, t - 1:t]
        m = m_ref[hh:hh + 1, 0:1]

        dmat = jnp.where(causal, bt_c - bt_r + it_r, -jnp.inf)
        inter = bt_c + m
        m_row = jnp.maximum(inter, jnp.max(dmat, axis=1, keepdims=True))
        w_intra = jnp.exp(dmat - m_row)
        w_inter = jnp.exp(inter - m_row)
        qk = jnp.dot(q, kt, preferred_element_type=F32)
        a = (qk * w_intra).astype(BF16)
        ct = ct_ref[hh]
        nd = (jnp.dot(a, vext, preferred_element_type=F32)
              + w_inter * jnp.dot(q, ct.astype(BF16), preferred_element_type=F32))
        num = nd[:, :ML_DV]
        den = nd[:, ML_DV:ML_DV + 1]
        denom = jnp.maximum(jnp.abs(den), jnp.exp(-m_row))
        hv = num / denom

        src = g - bt_r + it_r
        m_new = jnp.maximum(g + m, jnp.max(src, axis=1, keepdims=True))
        w_src = jnp.exp(src - m_new)
        decay = jnp.exp(g + m - m_new)
        ks = (kt.astype(F32) * w_src).astype(BF16)
        ct_ref[hh] = decay * ct + jnp.dot(ks, vext, preferred_element_type=F32)
        m_ref[hh:hh + 1, :] = jnp.broadcast_to(m_new, (1, LANE))

        hv = hv * lax.rsqrt(jnp.mean(hv * hv, axis=-1, keepdims=True) + EPS)
        hv = hv * hn_ref[:, hh * ML_DV:(hh + 1) * ML_DV]
        og = _sigmoid(om_ref[0, :, hh * ML_DV:(hh + 1) * ML_DV].astype(F32))
        o_ref[0, :, hh * ML_DV:(hh + 1) * ML_DV] = (og * hv).astype(o_ref.dtype)


def _mlstm(qm, kmt, proj3, cols, gcol, grow, hnorm):
    b, lp, qw = qm.shape
    nc = lp // CHUNK
    vw = HEADS * ML_DV
    v_cb = cols["vm"] // vw
    om_cb = cols["om"] // vw
    return pl.pallas_call(
        _mlstm_kernel,
        grid=(b, nc),
        in_specs=[
            pl.BlockSpec((1, CHUNK, qw), lambda bi, c: (bi, c, 0)),
            pl.BlockSpec((1, qw, CHUNK), lambda bi, c: (bi, 0, c)),
            pl.BlockSpec((1, CHUNK, vw), lambda bi, c: (bi, c, v_cb)),
            pl.BlockSpec((1, CHUNK, vw), lambda bi, c: (bi, c, om_cb)),
            pl.BlockSpec((1, CHUNK, LANE), lambda bi, c: (bi, c, 0)),
            pl.BlockSpec((1, 3 * HEADS, CHUNK), lambda bi, c: (bi, 0, c)),
            pl.BlockSpec((1, vw), lambda bi, c: (0, 0)),
        ],
        out_specs=pl.BlockSpec((1, CHUNK, vw), lambda bi, c: (bi, c, 0)),
        out_shape=jax.ShapeDtypeStruct((b, lp, vw), BF16),
        scratch_shapes=[pltpu.VMEM((HEADS, ML_DK, ML_DV + LANE), F32), pltpu.VMEM((HEADS, LANE), F32)],
        compiler_params=_cparams(("parallel", "arbitrary")),
        name="mlstm",
    )(qm, kmt, proj3, proj3, gcol, grow, hnorm)


def _merge_kernel(ya_ref, ym_ref, wa_ref, wm_ref, ga_ref, gm_ref, o_ref):
    a = jnp.dot(ya_ref[...], wa_ref[...], preferred_element_type=F32)
    m = jnp.dot(ym_ref[...], wm_ref[...], preferred_element_type=F32)
    out = _sigmoid(ga_ref[...].astype(F32)) * a + _sigmoid(gm_ref[...].astype(F32)) * m
    o_ref[...] = out.astype(o_ref.dtype)


def _merge(y_att, y_ml, wa, wm, proj, cols):
    n, ka = y_att.shape
    km = y_ml.shape[1]
    d = wa.shape[1]
    tm = _divisor_tile(n, 1024, SUBLANE)
    tn = _divisor_tile(d, 1024, LANE)
    ga_cb = cols["ga"] // tn
    gm_cb = cols["gm"] // tn
    return pl.pallas_call(
        _merge_kernel,
        grid=(n // tm, d // tn),
        in_specs=[
            pl.BlockSpec((tm, ka), lambda i, j: (i, 0)),
            pl.BlockSpec((tm, km), lambda i, j: (i, 0)),
            pl.BlockSpec((ka, tn), lambda i, j: (0, j)),
            pl.BlockSpec((km, tn), lambda i, j: (0, j)),
            pl.BlockSpec((tm, tn), lambda i, j: (i, ga_cb + j)),
            pl.BlockSpec((tm, tn), lambda i, j: (i, gm_cb + j)),
        ],
        out_specs=pl.BlockSpec((tm, tn), lambda i, j: (i, j)),
        out_shape=jax.ShapeDtypeStruct((n, d), BF16),
        compiler_params=_cparams(("parallel", "arbitrary")),
        name="merge",
    )(y_att, y_ml, wa, wm, proj, proj)


def _out_residual_kernel(x_ref, w_ref, h_ref, o_ref, *, tm, lp, npad, nbatch):
    mix = jnp.dot(x_ref[...], w_ref[...], preferred_element_type=F32)
    valid = _row_valid(pl.program_id(0) * tm, tm, lp, npad, nbatch)
    o_ref[...] = h_ref[...] + jnp.where(valid, mix, 0.0)


def _out_residual(x, w, h, lp, npad, nbatch):
    n, k = x.shape
    d = w.shape[1]
    tm = _divisor_tile(n, 1024, SUBLANE)
    tn = _divisor_tile(d, 1024, LANE)
    return pl.pallas_call(
        functools.partial(_out_residual_kernel, tm=tm, lp=lp, npad=npad, nbatch=nbatch),
        grid=(n // tm, d // tn),
        in_specs=[
            pl.BlockSpec((tm, k), lambda i, j: (i, 0)),
            pl.BlockSpec((k, tn), lambda i, j: (0, j)),
            pl.BlockSpec((tm, tn), lambda i, j: (i, j)),
        ],
        out_specs=pl.BlockSpec((tm, tn), lambda i, j: (i, j)),
        out_shape=jax.ShapeDtypeStruct((n, d), F32),
        compiler_params=_cparams(("parallel", "arbitrary")),
        name="out_residual",
    )(x, w, h)


def _ffn_kernel(x_ref, g_ref, wg_ref, wu_ref, wd_ref, o_ref, xn_ref, *, tm, lp, npad, nbatch):
    f = pl.program_id(1)

    @pl.when(f == 0)
    def _():
        x = x_ref[...]
        ms = jnp.mean(x * x, axis=-1, keepdims=True)
        xn_ref[...] = (x * lax.rsqrt(ms + EPS) * g_ref[...]).astype(BF16)
        o_ref[...] = jnp.zeros_like(o_ref)

    xn = xn_ref[...]
    gate = jnp.dot(xn, wg_ref[...], preferred_element_type=F32)
    up = jnp.dot(xn, wu_ref[...], preferred_element_type=F32)
    act = (gate * _sigmoid(gate) * up).astype(BF16)
    o_ref[...] += jnp.dot(act, wd_ref[...], preferred_element_type=F32)

    @pl.when(f == pl.num_programs(1) - 1)
    def _():
        valid = _row_valid(pl.program_id(0) * tm, tm, lp, npad, nbatch)
        o_ref[...] = x_ref[...] + jnp.where(valid, o_ref[...], 0.0)


def _ffn(h, gain, wg, wu, wd, lp, npad, nbatch):
    n, d = h.shape
    dff = wg.shape[1]
    tm = _divisor_tile(n, 1024, SUBLANE)
    tf = _divisor_tile(dff, 512, LANE)
    return pl.pallas_call(
        functools.partial(_ffn_kernel, tm=tm, lp=lp, npad=npad, nbatch=nbatch),
        grid=(n // tm, dff // tf),
        in_specs=[
            pl.BlockSpec((tm, d), lambda i, f: (i, 0)),
            pl.BlockSpec((1, d), lambda i, f: (0, 0)),
            pl.BlockSpec((d, tf), lambda i, f: (0, f)),
            pl.BlockSpec((d, tf), lambda i, f: (0, f)),
            pl.BlockSpec((tf, d), lambda i, f: (f, 0)),
        ],
        out_specs=pl.BlockSpec((tm, d), lambda i, f: (i, 0)),
        out_shape=jax.ShapeDtypeStruct((n, d), F32),
        scratch_shapes=[pltpu.VMEM((tm, d), BF16)],
        compiler_params=_cparams(("parallel", "arbitrary")),
        name="ffn",
    )(h, gain, wg, wu, wd)


def _bf16_bits(x):
    bits = lax.bitcast_convert_type(x, jnp.uint32)
    rounded = bits + jnp.uint32(0x7FFF) + ((bits >> 16) & jnp.uint32(1))
    return rounded >> 16


def _router_kernel(x_ref, g_ref, rw_ref, xp_ref, route_ref):
    x = x_ref[...]
    d = x.shape[1]
    ms = jnp.mean(x * x, axis=-1, keepdims=True)
    xn = x * lax.rsqrt(ms + EPS) * g_ref[...]
    xp_ref[...] = _bf16_bits(xn[:, :d // 2]) | (_bf16_bits(xn[:, d // 2:]) << 16)

    logits = _dot_f32_3pass(xn, rw_ref[...])
    lane = lax.broadcasted_iota(jnp.int32, logits.shape, 1)
    logits = jnp.where(lane < N_EXPERTS, logits, -jnp.inf)
    v1 = jnp.max(logits, axis=-1, keepdims=True)
    i1 = jnp.min(jnp.where(logits == v1, lane, LANE), axis=-1, keepdims=True)
    rest = jnp.where(lane == i1, -jnp.inf, logits)
    v2 = jnp.max(rest, axis=-1, keepdims=True)
    i2 = jnp.min(jnp.where(rest == v2, lane, LANE), axis=-1, keepdims=True)
    e = jnp.exp(v2 - v1)
    g1 = 1.0 / (1.0 + e)
    g2 = e / (1.0 + e)
    route = jnp.where(lane == 0, i1.astype(F32),
                      jnp.where(lane == 1, i2.astype(F32),
                                jnp.where(lane == 2, g1, jnp.where(lane == 3, g2, 0.0))))
    route_ref[...] = route


def _router(h, gain, rw):
    n, d = h.shape
    tm = _divisor_tile(n, 1024, SUBLANE)
    return pl.pallas_call(
        _router_kernel,
        grid=(n // tm,),
        in_specs=[
            pl.BlockSpec((tm, d), lambda i: (i, 0)),
            pl.BlockSpec((1, d), lambda i: (0, 0)),
            pl.BlockSpec((d, LANE), lambda i: (0, 0)),
        ],
        out_specs=[
            pl.BlockSpec((tm, d // 2), lambda i: (i, 0)),
            pl.BlockSpec((tm, LANE), lambda i: (i, 0)),
        ],
        out_shape=[jax.ShapeDtypeStruct((n, d // 2), jnp.uint32), jax.ShapeDtypeStruct((n, LANE), F32)],
        compiler_params=_cparams(("parallel",)),
        name="router",
    )(h, gain, rw)


GATHER_ROWS = 512


def _row_copy(src_hbm, src_row, dst_ref, dst_row, sem):
    return pltpu.make_async_copy(src_hbm.at[pl.ds(src_row, 1)], dst_ref.at[pl.ds(dst_row, 1)], sem)


def _gather_kernel(idx_ref, x_hbm, o_ref, sem, *, rows):
    def start(r2, carry):
        for u in range(2):
            r = 2 * r2 + u
            _row_copy(x_hbm, idx_ref[0, 0, r], o_ref, r, sem).start(priority=u)
        return carry

    lax.fori_loop(0, rows // 2, start, 0, unroll=4)
    pltpu.make_async_copy(x_hbm.at[pl.ds(0, rows)], o_ref, sem).wait()


def _moe_gather(xp, slot_tok, rows):
    p = slot_tok.shape[0]
    w = xp.shape[1]
    nb = p // rows
    idx3 = slot_tok.reshape(nb, 1, rows)
    return pl.pallas_call(
        functools.partial(_gather_kernel, rows=rows),
        grid=(nb,),
        in_specs=[
            pl.BlockSpec((1, 1, rows), lambda i: (i, 0, 0), memory_space=pltpu.SMEM),
            pl.BlockSpec(memory_space=pl.ANY),
        ],
        out_specs=pl.BlockSpec((rows, w), lambda i: (i, 0)),
        out_shape=jax.ShapeDtypeStruct((p, w), xp.dtype),
        scratch_shapes=[pltpu.SemaphoreType.DMA(())],
        compiler_params=_cparams(("arbitrary",)),
        name="moe_gather",
    )(idx3, xp)


def _moe_ffn_kernel(blk_e_ref, nblk_ref, xg_ref, wg_ref, wu_ref, wd_ref, o_ref, xn_ref):
    i = pl.program_id(0)
    f = pl.program_id(1)

    @pl.when(i < nblk_ref[0])
    def _():
        @pl.when(f == 0)
        def _():
            packed = xg_ref[...]
            half = packed.shape[1]
            lo = lax.bitcast_convert_type(packed << 16, F32)
            hi = lax.bitcast_convert_type(packed & jnp.uint32(0xFFFF0000), F32)
            xn_ref[:, :half] = lo.astype(BF16)
            xn_ref[:, half:] = hi.astype(BF16)
            o_ref[...] = jnp.zeros_like(o_ref)

        xn = xn_ref[...]
        gate = jnp.dot(xn, wg_ref[0], preferred_element_type=F32)
        up = jnp.dot(xn, wu_ref[0], preferred_element_type=F32)
        act = (gate * _sigmoid(gate) * up).astype(BF16)
        o_ref[...] += jnp.dot(act, wd_ref[0], preferred_element_type=F32)

    @pl.when((i >= nblk_ref[0]) & (f == 0))
    def _():
        o_ref[...] = jnp.zeros_like(o_ref)


def _moe_ffn(xg, blk_e, nblk, wg, wu, wd, tm):
    p, half = xg.shape
    d = 2 * half
    dff = wg.shape[2]
    tf = _divisor_tile(dff, 512, LANE)
    nf = dff // tf
    nb = p // tm

    def row_map(i, f, be, nu):
        return (jnp.minimum(i, nu[0] - 1), 0)

    def f_eff(i, f, nu):
        return jnp.where(i < nu[0], f, nf - 1)

    grid_spec = pltpu.PrefetchScalarGridSpec(
        num_scalar_prefetch=2,
        grid=(nb, nf),
        in_specs=[
            pl.BlockSpec((tm, half), row_map),
            pl.BlockSpec((1, d, tf), lambda i, f, be, nu: (be[i], 0, f_eff(i, f, nu))),
            pl.BlockSpec((1, d, tf), lambda i, f, be, nu: (be[i], 0, f_eff(i, f, nu))),
            pl.BlockSpec((1, tf, d), lambda i, f, be, nu: (be[i], f_eff(i, f, nu), 0)),
        ],
        out_specs=pl.BlockSpec((tm, d), lambda i, f, be, nu: (i, 0)),
        scratch_shapes=[pltpu.VMEM((tm, d), BF16)],
    )
    return pl.pallas_call(
        _moe_ffn_kernel,
        grid_spec=grid_spec,
        out_shape=jax.ShapeDtypeStruct((p, d), F32),
        compiler_params=_cparams(("arbitrary", "arbitrary")),
        name="moe_ffn",
    )(blk_e, nblk, xg, wg, wu, wd)


def _combine_kernel(d_ref, route_ref, h_ref, y_hbm, o_ref, y0_ref, y1_ref, sem, *, rows, lp, npad, nbatch):
    def start(r, carry):
        _row_copy(y_hbm, d_ref[0, 0, 2 * r], y0_ref, r, sem).start(priority=0)
        _row_copy(y_hbm, d_ref[0, 0, 2 * r + 1], y1_ref, r, sem).start(priority=1)
        return carry

    lax.fori_loop(0, rows, start, 0, unroll=4)
    pltpu.make_async_copy(y_hbm.at[pl.ds(0, rows)], y0_ref, sem).wait()
    pltpu.make_async_copy(y_hbm.at[pl.ds(0, rows)], y1_ref, sem).wait()

    g0 = route_ref[:, 2:3]
    g1 = route_ref[:, 3:4]
    y = g0 * y0_ref[...] + g1 * y1_ref[...]
    valid = _row_valid(pl.program_id(0) * rows, rows, lp, npad, nbatch)
    o_ref[...] = h_ref[...] + jnp.where(valid, y, 0.0)


def _moe_combine(h, route, dest, yg, rows, lp, npad, nbatch):
    n, d = h.shape
    nb = n // rows
    d3 = dest.reshape(nb, 1, 2 * rows)
    return pl.pallas_call(
        functools.partial(_combine_kernel, rows=rows, lp=lp, npad=npad, nbatch=nbatch),
        grid=(nb,),
        in_specs=[
            pl.BlockSpec((1, 1, 2 * rows), lambda i: (i, 0, 0), memory_space=pltpu.SMEM),
            pl.BlockSpec((rows, LANE), lambda i: (i, 0)),
            pl.BlockSpec((rows, d), lambda i: (i, 0)),
            pl.BlockSpec(memory_space=pl.ANY),
        ],
        out_specs=pl.BlockSpec((rows, d), lambda i: (i, 0)),
        out_shape=jax.ShapeDtypeStruct((n, d), F32),
        scratch_shapes=[pltpu.VMEM((rows, d), F32), pltpu.VMEM((rows, d), F32), pltpu.SemaphoreType.DMA(())],
        compiler_params=_cparams(("arbitrary",)),
        name="moe_combine",
    )(d3, route, h, yg)


def _moe_layout(experts, tm, nblocks):
    n = experts.shape[0]
    flat_e = experts.reshape(-1)
    onehot = (flat_e[:, None] == jnp.arange(N_EXPERTS, dtype=jnp.int32)[None, :]).astype(jnp.int32)
    csum = jnp.cumsum(onehot, axis=0)
    pos = jnp.take_along_axis(csum, flat_e[:, None], axis=1)[:, 0] - 1
    counts = csum[-1]
    padded = (counts + tm - 1) // tm * tm
    ends = jnp.cumsum(padded)
    gstart = ends - padded
    dest = (gstart[flat_e] + pos).astype(jnp.int32)
    flat_tok = jnp.arange(2 * n, dtype=jnp.int32) // 2
    slot_tok = jnp.zeros((nblocks * tm,), jnp.int32).at[dest].set(flat_tok)
    nblk = (ends[-1] // tm).astype(jnp.int32)
    blk = jnp.minimum(jnp.arange(nblocks, dtype=jnp.int32), nblk - 1)
    blk_e = jnp.minimum(jnp.searchsorted(ends, blk * tm, side="right"), N_EXPERTS - 1).astype(jnp.int32)
    return dest, slot_tok, blk_e, nblk.reshape(1)


def _moe_block(h, gain, rw, wg, wu, wd, lp, npad, nbatch):
    n, d = h.shape
    xp, route = _router(h, gain, rw)
    experts = route[:, :2].astype(jnp.int32)
    tm = _divisor_tile(n, 1024, GATHER_ROWS) if n % GATHER_ROWS == 0 else n
    rows = min(GATHER_ROWS, tm)
    nblocks = -(-(2 * n + N_EXPERTS * (tm - 1)) // tm)
    dest, slot_tok, blk_e, nblk = _moe_layout(experts, tm, nblocks)
    xg = _moe_gather(xp, slot_tok, rows)
    yg = _moe_ffn(xg, blk_e, nblk, wg, wu, wd, tm)
    return _moe_combine(h, route, dest, yg, rows, lp, npad, nbatch)


def _inproj_layout(d_model):
    att_w = HEADS * ATT_DH
    qk_w = HEADS * ML_DK
    v_w = HEADS * ML_DV
    names = ("qa", "ka", "va", "fa", "qm", "km", "vm", "im", "fm", "om", "ga", "gm")
    widths = (att_w, att_w, att_w, HEADS, qk_w, qk_w, v_w, HEADS, HEADS, v_w, d_model, d_model)
    src, off = {}, 0
    for nm, wd in zip(names, widths):
        src[nm] = (off, wd)
        off += wd
    big_order = ("vm", "om", "ga", "gm", "qa", "ka", "va", "qm", "km")
    cols, off = {}, 0
    for nm in big_order:
        cols[nm] = off
        off += src[nm][1]
    return src, big_order, cols


def kernel(x, meta_tokens, ln_mix, w_in, att_q_norm, att_k_norm, att_f_bias, ml_conv_w, ml_conv_b, ml_i_bias,
           ml_f_bias, ml_h_norm, w_branch_att, w_branch_ml, w_out, ln_ffn, dense_w_gate, dense_w_up, dense_w_down,
           router_w, moe_w_gate, moe_w_up, moe_w_down):
    nbatch, seq, d = x.shape
    depth = w_in.shape[0]
    lp = -(-(CHUNK + seq) // SEQ_TILE) * SEQ_TILE
    npad = lp - seq - N_META
    n = nbatch * lp
    src, big_order, cols = _inproj_layout(d)

    h = jnp.concatenate([
        jnp.zeros((nbatch, npad, d), x.dtype),
        jnp.broadcast_to(meta_tokens[None].astype(x.dtype), (nbatch, N_META, d)),
        x], axis=1).reshape(n, d)

    for layer in range(depth):
        wl = w_in[layer]
        w_big = jnp.concatenate([wl[:, src[nm][0]:src[nm][0] + src[nm][1]] for nm in big_order], axis=1).astype(BF16)
        w_gates = jnp.concatenate(
            [wl[:, src[nm][0]:src[nm][0] + HEADS] for nm in ("fa", "fm", "im")]
            + [jnp.zeros((d, LANE - 3 * HEADS), F32)], axis=1)
        gate_bias = jnp.concatenate([att_f_bias[layer], ml_f_bias[layer], ml_i_bias[layer],
                                     jnp.zeros((LANE - 3 * HEADS,), F32)]).reshape(1, LANE)

        proj, gates = _norm_inproj(h, ln_mix[layer].reshape(1, d), w_big, w_gates)
        proj3 = proj.reshape(nbatch, lp, -1)
        gcol, grow = _gate_prep(gates.reshape(nbatch, lp, LANE), gate_bias, npad)
        qa, ka, qm, kmt = _qk_prep(proj3, cols, att_q_norm[layer].reshape(1, ATT_DH),
                                   att_k_norm[layer].reshape(1, ATT_DH), ml_conv_w[layer],
                                   ml_conv_b[layer].reshape(1, -1), npad)
        c_rows = grow[:, :HEADS, :].reshape(nbatch * HEADS, lp // SEQ_TILE, 1, SEQ_TILE)
        y_att = _fox_attention(qa, ka, proj3, cols["va"], c_rows, npad)
        y_ml = _mlstm(qm, kmt, proj3, cols, gcol, grow, ml_h_norm[layer].reshape(1, -1))
        merged = _merge(y_att.reshape(n, -1), y_ml.reshape(n, -1), w_branch_att[layer].astype(BF16),
                        w_branch_ml[layer].astype(BF16), proj, cols)
        h = _out_residual(merged, w_out[layer].astype(BF16), h, lp, npad, nbatch)

        j = layer // 2
        gain = ln_ffn[layer].reshape(1, d)
        if layer % 2 == 0:
            h = _ffn(h, gain, dense_w_gate[j].astype(BF16), dense_w_up[j].astype(BF16),
                     dense_w_down[j].astype(BF16), lp, npad, nbatch)
        else:
            rw = jnp.concatenate([router_w[j], jnp.zeros((d, LANE - N_EXPERTS), F32)], axis=1)
            h = _moe_block(h, gain, rw, moe_w_gate[j].astype(BF16), moe_w_up[j].astype(BF16),
                           moe_w_down[j].astype(BF16), lp, npad, nbatch)

    return h.reshape(nbatch, lp, d)[:, lp - seq:, :]
```

```python
import functools

import jax
import jax.numpy as jnp
from jax import lax
from jax.experimental import pallas as pl
from jax.experimental.pallas import tpu as pltpu

N_META = 16
CHUNK = 128
HEADS = 8
ATT_DH = 128
ML_DK = 128
ML_DV = 256
CONV_WIDTH = 4
N_EXPERTS = 8
EPS = 1e-6
NEG = -1e30
LOG2E = 1.4426950408889634

LANE = 128
SUBLANE = 8
VMEM_BYTES_V7X = 64 * 1024 * 1024
VMEM_LIMIT = 56 * 1024 * 1024

SEQ_TILE = 512
HALO = SUBLANE

F32 = jnp.float32
BF16 = jnp.bfloat16
HIGHEST = lax.Precision.HIGHEST


def _divisor_tile(n, pref, align):
    if n <= pref:
        return n
    t = (pref // align) * align
    while t >= align:
        if n % t == 0:
            return t
        t -= align
    return n


def _cparams(sem):
    return pltpu.CompilerParams(dimension_semantics=sem, vmem_limit_bytes=VMEM_LIMIT)


def _row_valid(row0, nrows, lp, npad, nbatch):
    r = row0 + lax.broadcasted_iota(jnp.int32, (nrows, 1), 0)
    invalid = (r >= 0) & (r < npad)
    for b in range(1, nbatch):
        invalid = invalid | ((r >= b * lp) & (r < b * lp + npad))
    return jnp.logical_not(invalid)


def _log_sigmoid(x):
    return jnp.minimum(x, 0.0) - jnp.log1p(jnp.exp(-jnp.abs(x)))


def _sigmoid(x):
    return 1.0 / (1.0 + jnp.exp(-x))


def _dot_f32_3pass(x, w):
    xh = x.astype(BF16)
    xl = (x - xh.astype(F32)).astype(BF16)
    wh = w.astype(BF16)
    wl = (w - wh.astype(F32)).astype(BF16)
    return (jnp.dot(xh, wh, preferred_element_type=F32) + jnp.dot(xl, wh, preferred_element_type=F32)
            + jnp.dot(xh, wl, preferred_element_type=F32))


def _norm_inproj_kernel(x_ref, g_ref, w_ref, wg_ref, o_ref, og_ref, xn_ref):
    @pl.when(pl.program_id(1) == 0)
    def _():
        x = x_ref[...]
        ms = jnp.mean(x * x, axis=-1, keepdims=True)
        xn = x * lax.rsqrt(ms + EPS) * g_ref[...]
        xn_ref[...] = xn.astype(BF16)
        og_ref[...] = _dot_f32_3pass(xn, wg_ref[...])

    o_ref[...] = jnp.dot(xn_ref[...], w_ref[...], preferred_element_type=F32).astype(o_ref.dtype)


def _norm_inproj(h, gain, w_big, w_gates):
    n, d = h.shape
    wn = w_big.shape[1]
    tm = _divisor_tile(n, 1024, SUBLANE)
    tn = _divisor_tile(wn, 1024, LANE)
    return pl.pallas_call(
        _norm_inproj_kernel,
        grid=(n // tm, wn // tn),
        in_specs=[
            pl.BlockSpec((tm, d), lambda i, j: (i, 0)),
            pl.BlockSpec((1, d), lambda i, j: (0, 0)),
            pl.BlockSpec((d, tn), lambda i, j: (0, j)),
            pl.BlockSpec((d, LANE), lambda i, j: (0, 0)),
        ],
        out_specs=[
            pl.BlockSpec((tm, tn), lambda i, j: (i, j)),
            pl.BlockSpec((tm, LANE), lambda i, j: (i, 0)),
        ],
        out_shape=[jax.ShapeDtypeStruct((n, wn), BF16), jax.ShapeDtypeStruct((n, LANE), F32)],
        scratch_shapes=[pltpu.VMEM((tm, d), BF16)],
        compiler_params=_cparams(("parallel", "arbitrary")),
        name="norm_inproj",
    )(h, gain, w_big, w_gates)


def _gate_prep_kernel(g_ref, b_ref, col_ref, row_ref, carry_ref, *, npad):
    c = pl.program_id(1)

    @pl.when(c == 0)
    def _():
        carry_ref[...] = jnp.zeros_like(carry_ref)

    x = g_ref[0] + b_ref[...]
    lane = lax.broadcasted_iota(jnp.int32, (CHUNK, LANE), 1)
    pos = c * CHUNK + lax.broadcasted_iota(jnp.int32, (CHUNK, LANE), 0)
    valid = pos >= npad
    ls = _log_sigmoid(x)
    z = jnp.where(lane < HEADS, ls, jnp.where((lane < 2 * HEADS) & valid, ls, 0.0))
    r_i = lax.broadcasted_iota(jnp.int32, (CHUNK, CHUNK), 0)
    c_i = lax.broadcasted_iota(jnp.int32, (CHUNK, CHUNK), 1)
    tril = (r_i >= c_i).astype(F32)
    s = jnp.dot(tril, z, preferred_element_type=F32, precision=HIGHEST)
    s = s + jnp.where(lane < HEADS, carry_ref[0:1, :], 0.0)
    carry_ref[...] = jnp.broadcast_to(s[CHUNK - 1:CHUNK, :], carry_ref.shape)
    ipre = jnp.where(valid, x, -jnp.inf)
    col = jnp.where(lane < HEADS, s * LOG2E,
                    jnp.where(lane < 2 * HEADS, s, jnp.where(lane < 3 * HEADS, ipre, 0.0)))
    col_ref[0] = col
    row_ref[0] = col.T[:3 * HEADS, :]


def _gate_prep(gates3, bias, npad):
    b, lp, _ = gates3.shape
    nc = lp // CHUNK
    return pl.pallas_call(
        functools.partial(_gate_prep_kernel, npad=npad),
        grid=(b, nc),
        in_specs=[
            pl.BlockSpec((1, CHUNK, LANE), lambda bi, c: (bi, c, 0)),
            pl.BlockSpec((1, LANE), lambda bi, c: (0, 0)),
        ],
        out_specs=[
            pl.BlockSpec((1, CHUNK, LANE), lambda bi, c: (bi, c, 0)),
            pl.BlockSpec((1, 3 * HEADS, CHUNK), lambda bi, c: (bi, 0, c)),
        ],
        out_shape=[jax.ShapeDtypeStruct((b, lp, LANE), F32), jax.ShapeDtypeStruct((b, 3 * HEADS, lp), F32)],
        scratch_shapes=[pltpu.VMEM((SUBLANE, LANE), F32)],
        compiler_params=_cparams(("parallel", "arbitrary")),
        name="gate_prep",
    )(gates3, bias)


def _qk_prep_kernel(qa_ref, ka_ref, qm_ref, km_ref, qmh_ref, kmh_ref, gq_ref, gk_ref, cw_ref, cb_ref,
                    qa_o, ka_o, qm_o, kmt_o, win_ref, *, npad):
    c = pl.program_id(1)
    ml_w = HEADS * ML_DK

    for hh in range(HEADS):
        sl = slice(hh * ATT_DH, (hh + 1) * ATT_DH)
        q = qa_ref[0, :, sl].astype(F32)
        q = q * lax.rsqrt(jnp.mean(q * q, axis=-1, keepdims=True) + EPS) * gq_ref[...]
        qa_o[0, :, sl] = (q * (ATT_DH ** -0.5 * LOG2E)).astype(BF16)
        k = ka_ref[0, :, sl].astype(F32)
        k = k * lax.rsqrt(jnp.mean(k * k, axis=-1, keepdims=True) + EPS) * gk_ref[...]
        ka_o[0, :, sl] = k.astype(BF16)

    pos = c * CHUNK + lax.broadcasted_iota(jnp.int32, (CHUNK, 1), 0)
    hpos = c * CHUNK - HALO + lax.broadcasted_iota(jnp.int32, (HALO, 1), 0)
    row_ok = pos >= npad
    halo_ok = (hpos >= npad) & (c > 0)

    def conv_silu(x_ref, xh_ref, col0):
        win_ref[0:HALO, :] = jnp.where(halo_ok, xh_ref[0].astype(F32), 0.0)
        win_ref[HALO:HALO + CHUNK, :] = jnp.where(row_ok, x_ref[0].astype(F32), 0.0)
        y = jnp.broadcast_to(cb_ref[:, col0:col0 + ml_w], (CHUNK, ml_w))
        for j in range(CONV_WIDTH):
            off = HALO - (CONV_WIDTH - 1) + j
            y = y + cw_ref[j:j + 1, col0:col0 + ml_w] * win_ref[off:off + CHUNK, :]
        return y * _sigmoid(y)

    yq = conv_silu(qm_ref, qmh_ref, 0)
    qm_o[0] = yq.astype(BF16)
    yk = conv_silu(km_ref, kmh_ref, ml_w) * (ML_DK ** -0.5)
    for hh in range(HEADS):
        sl = slice(hh * ML_DK, (hh + 1) * ML_DK)
        kmt_o[0, sl, :] = yk[:, sl].T.astype(BF16)


def _qk_prep(proj3, cols, gq, gk, conv_w, conv_b, npad):
    b, lp, _ = proj3.shape
    nc = lp // CHUNK
    w = HEADS * ATT_DH
    qa_c, ka_c, qm_c, km_c = (cols[k] // w for k in ("qa", "ka", "qm", "km"))
    hpc = CHUNK // HALO

    def main(cb):
        return pl.BlockSpec((1, CHUNK, w), lambda bi, c: (bi, c, cb))

    def halo(cb):
        return pl.BlockSpec((1, HALO, w), lambda bi, c: (bi, jnp.maximum(c * hpc - 1, 0), cb))

    full = lambda shape: pl.BlockSpec(shape, lambda bi, c: (0,) * len(shape))
    return pl.pallas_call(
        functools.partial(_qk_prep_kernel, npad=npad),
        grid=(b, nc),
        in_specs=[main(qa_c), main(ka_c), main(qm_c), main(km_c), halo(qm_c), halo(km_c),
                  full((1, ATT_DH)), full((1, ATT_DH)), full((CONV_WIDTH, 2 * w)), full((1, 2 * w))],
        out_specs=[
            pl.BlockSpec((1, CHUNK, w), lambda bi, c: (bi, c, 0)),
            pl.BlockSpec((1, CHUNK, w), lambda bi, c: (bi, c, 0)),
            pl.BlockSpec((1, CHUNK, w), lambda bi, c: (bi, c, 0)),
            pl.BlockSpec((1, w, CHUNK), lambda bi, c: (bi, 0, c)),
        ],
        out_shape=[jax.ShapeDtypeStruct((b, lp, w), BF16)] * 3 + [jax.ShapeDtypeStruct((b, w, lp), BF16)],
        scratch_shapes=[pltpu.VMEM((HALO + CHUNK, w), F32)],
        compiler_params=_cparams(("parallel", "arbitrary")),
        name="qk_prep",
    )(proj3, proj3, proj3, proj3, proj3, proj3, gq, gk, conv_w, conv_b)


FOX_PART_ROWS = 256
FOX_Q_TILE_MAX = 1536


def _fox_kernel(q_ref, k_ref, v_ref, c_ref, o_ref, m_ref, acc_ref, s0_ref, s1_ref, *, tq, tk, npad):
    i = pl.program_id(2)
    r = tq // tk
    hq = min(FOX_PART_ROWS, tk)
    slots = (s0_ref, s1_ref)
    c0 = c_ref[0, r * i][:, 0:1]
    m_ref[...] = jnp.full_like(m_ref, NEG)
    acc_ref[...] = jnp.zeros_like(acc_ref)
    ones = jnp.ones((tk, LANE), BF16)

    def kv_start(j):
        return j * tk if isinstance(j, int) else pl.multiple_of(j * tk, tk)

    def scores(j, s_ref, masked, row0):
        k = k_ref[0, pl.ds(kv_start(j), tk), :]
        s = lax.dot_general(q_ref[0, row0:, :], k, (((1,), (1,)), ((), ())), preferred_element_type=F32)
        s = s + (c0 - c_ref[0, j])
        if masked:
            qpos = i * tq + row0 + lax.broadcasted_iota(jnp.int32, (tq - row0, tk), 0)
            kpos = j * tk + lax.broadcasted_iota(jnp.int32, (tq - row0, tk), 1)
            s = jnp.where((kpos <= qpos) & (kpos >= npad), s, NEG)
        s_ref[row0:, :] = s

    def accumulate(j, s_ref, row0):
        vext = jnp.concatenate([v_ref[0, pl.ds(kv_start(j), tk), :], ones], axis=1)
        for part in range(row0 // hq, tq // hq):
            rows = slice(part * hq, (part + 1) * hq)
            s = s_ref[rows, :]
            m_prev = m_ref[rows, :]
            m_new = jnp.maximum(m_prev, jnp.broadcast_to(jnp.max(s, axis=1, keepdims=True), (hq, LANE)))
            alpha = jnp.exp2(m_prev - m_new)
            p = jnp.exp2(s - jnp.concatenate([m_new] * (tk // LANE), axis=1))
            pv = jnp.dot(p.astype(BF16), vext, preferred_element_type=F32)
            acc_ref[rows, :] = jnp.concatenate([alpha, alpha], axis=1) * acc_ref[rows, :] + pv
            m_ref[rows, :] = m_new

    def run(chunks):
        for n, (j, d) in enumerate(chunks):
            if n + 1 < len(chunks):
                jn, dn = chunks[n + 1]
                scores(jn, slots[(n + 1) % 2], dn is not None, 0 if dn is None else dn * tk)
            accumulate(j, slots[n % 2], 0 if d is None else d * tk)

    first = r * i
    diag = [(first + d, d) for d in range(r)]
    scores(0, s0_ref, True, 0)

    n_plain = jnp.maximum(first - 1, 0)
    odd = lax.bitwise_and(n_plain, 1)

    def pair_body(p, carry):
        j = 2 * p
        scores(j + 1, s1_ref, False, 0)
        accumulate(j, s0_ref, 0)
        scores(j + 2, s0_ref, False, 0)
        accumulate(j + 1, s1_ref, 0)
        return carry

    lax.fori_loop(0, lax.shift_right_logical(n_plain, 1), pair_body, 0)

    @pl.when(i == 0)
    def _():
        run(diag)

    @pl.when((i > 0) & (odd == 0))
    def _():
        run([(first - 1, None)] + diag)

    @pl.when((i > 0) & (odd == 1))
    def _():
        run([(first - 2, None), (first - 1, None)] + diag)

    o_ref[0] = (acc_ref[:, :ATT_DH] / acc_ref[:, ATT_DH:]).astype(o_ref.dtype)


def _fox_attention(qa, ka, proj3, v_col, c_rows, npad):
    b, lp, w = qa.shape
    tk = SEQ_TILE
    tq = _divisor_tile(lp, FOX_Q_TILE_MAX, tk)
    nq = lp // tq
    nk = lp // tk
    vcb = v_col // ATT_DH
    return pl.pallas_call(
        functools.partial(_fox_kernel, tq=tq, tk=tk, npad=npad),
        grid=(b, HEADS, nq),
        in_specs=[
            pl.BlockSpec((1, tq, ATT_DH), lambda bi, h, i: (bi, i, h)),
            pl.BlockSpec((1, lp, ATT_DH), lambda bi, h, i: (bi, 0, h)),
            pl.BlockSpec((1, lp, ATT_DH), lambda bi, h, i: (bi, 0, vcb + h)),
            pl.BlockSpec((1, nk, 1, tk), lambda bi, h, i: (bi * HEADS + h, 0, 0, 0)),
        ],
        out_specs=pl.BlockSpec((1, tq, ATT_DH), lambda bi, h, i: (bi, i, h)),
        out_shape=jax.ShapeDtypeStruct((b, lp, w), BF16),
        scratch_shapes=[pltpu.VMEM((tq, LANE), F32), pltpu.VMEM((tq, ATT_DH + LANE), F32),
                        pltpu.VMEM((tq, tk), F32), pltpu.VMEM((tq, tk), F32)],
        compiler_params=_cparams(("parallel", "parallel", "arbitrary")),
        name="fox_attention",
    )(qa, ka, proj3, c_rows)


def _mlstm_kernel(q_ref, kt_ref, v_ref, om_ref, col_ref, row_ref, hn_ref, o_ref, ct_ref, m_ref):
    c = pl.program_id(1)

    @pl.when(c == 0)
    def _():
        ct_ref[...] = jnp.zeros_like(ct_ref)
        m_ref[...] = jnp.zeros_like(m_ref)

    t = CHUNK
    r_i = lax.broadcasted_iota(jnp.int32, (t, t), 0)
    c_i = lax.broadcasted_iota(jnp.int32, (t, t), 1)
    causal = r_i >= c_i
    ones = jnp.ones((t, LANE), BF16)

    for hh in range(HEADS):
        q = q_ref[0, :, hh * ML_DK:(hh + 1) * ML_DK]
        kt = kt_ref[0, hh * ML_DK:(hh + 1) * ML_DK, :]
        v = v_ref[0, :, hh * ML_DV:(hh + 1) * ML_DV]
        vext = jnp.concatenate([v, ones], axis=1)
        bt_c = col_ref[0, :, HEADS + hh:HEADS + hh + 1]
        bt_r = row_ref[0, HEADS + hh:HEADS + hh + 1, :]
        it_r = row_ref[0, 2 * HEADS + hh:2 * HEADS + hh + 1, :]
        g = bt_r[:, t - 1:t]
        m = m_ref[hh, 0:1, 0:1]

        dmat = jnp.where(causal, bt_c - bt_r + it_r, -jnp.inf)
        inter = bt_c + m
        m_row = jnp.maximum(inter, jnp.max(dmat, axis=1, keepdims=True))
        w_intra = jnp.exp(dmat - m_row)
        w_inter = jnp.exp(inter - m_row)
        qk = jnp.dot(q, kt, preferred_element_type=F32)
        a = (qk * w_intra).astype(BF16)
        ct = ct_ref[hh]
        nd = (jnp.dot(a, vext, preferred_element_type=F32)
              + w_inter * jnp.dot(q, ct.astype(BF16), preferred_element_type=F32))
        num = nd[:, :ML_DV]
        den = nd[:, ML_DV:ML_DV + 1]
        denom = jnp.maximum(jnp.abs(den), jnp.exp(-m_row))
        hv = num / denom

        src = g - bt_r + it_r
        m_new = jnp.maximum(g + m, jnp.max(src, axis=1, keepdims=True))
        w_src = jnp.exp(src - m_new)
        decay = jnp.exp(g + m - m_new)
        ks = (kt.astype(F32) * w_src).astype(BF16)
        ct_ref[hh] = decay * ct + jnp.dot(ks, vext, preferred_element_type=F32)
        m_ref[hh] = jnp.broadcast_to(m_new, (SUBLANE, LANE))

        hv = hv * lax.rsqrt(jnp.mean(hv * hv, axis=-1, keepdims=True) + EPS)
        hv = hv * hn_ref[:, hh * ML_DV:(hh + 1) * ML_DV]
        og = _sigmoid(om_ref[0, :, hh * ML_DV:(hh + 1) * ML_DV].astype(F32))
        o_ref[0, :, hh * ML_DV:(hh + 1) * ML_DV] = (og * hv).astype(o_ref.dtype)


def _mlstm(qm, kmt, proj3, cols, gcol, grow, hnorm):
    b, lp, qw = qm.shape
    nc = lp // CHUNK
    vw = HEADS * ML_DV
    v_cb = cols["vm"] // vw
    om_cb = cols["om"] // vw
    return pl.pallas_call(
        _mlstm_kernel,
        grid=(b, nc),
        in_specs=[
            pl.BlockSpec((1, CHUNK, qw), lambda bi, c: (bi, c, 0)),
            pl.BlockSpec((1, qw, CHUNK), lambda bi, c: (bi, 0, c)),
            pl.BlockSpec((1, CHUNK, vw), lambda bi, c: (bi, c, v_cb)),
            pl.BlockSpec((1, CHUNK, vw), lambda bi, c: (bi, c, om_cb)),
            pl.BlockSpec((1, CHUNK, LANE), lambda bi, c: (bi, c, 0)),
            pl.BlockSpec((1, 3 * HEADS, CHUNK), lambda bi, c: (bi, 0, c)),
            pl.BlockSpec((1, vw), lambda bi, c: (0, 0)),
        ],
        out_specs=pl.BlockSpec((1, CHUNK, vw), lambda bi, c: (bi, c, 0)),
        out_shape=jax.ShapeDtypeStruct((b, lp, vw), BF16),
        scratch_shapes=[pltpu.VMEM((HEADS, ML_DK, ML_DV + LANE), F32), pltpu.VMEM((HEADS, SUBLANE, LANE), F32)],
        compiler_params=_cparams(("parallel", "arbitrary")),
        name="mlstm",
    )(qm, kmt, proj3, proj3, gcol, grow, hnorm)


def _merge_kernel(ya_ref, ym_ref, wa_ref, wm_ref, ga_ref, gm_ref, o_ref):
    a = jnp.dot(ya_ref[...], wa_ref[...], preferred_element_type=F32)
    m = jnp.dot(ym_ref[...], wm_ref[...], preferred_element_type=F32)
    out = _sigmoid(ga_ref[...].astype(F32)) * a + _sigmoid(gm_ref[...].astype(F32)) * m
    o_ref[...] = out.astype(o_ref.dtype)


def _merge(y_att, y_ml, wa, wm, proj, cols):
    n, ka = y_att.shape
    km = y_ml.shape[1]
    d = wa.shape[1]
    tm = _divisor_tile(n, 1024, SUBLANE)
    tn = _divisor_tile(d, 1024, LANE)
    ga_cb = cols["ga"] // tn
    gm_cb = cols["gm"] // tn
    return pl.pallas_call(
        _merge_kernel,
        grid=(n // tm, d // tn),
        in_specs=[
            pl.BlockSpec((tm, ka), lambda i, j: (i, 0)),
            pl.BlockSpec((tm, km), lambda i, j: (i, 0)),
            pl.BlockSpec((ka, tn), lambda i, j: (0, j)),
            pl.BlockSpec((km, tn), lambda i, j: (0, j)),
            pl.BlockSpec((tm, tn), lambda i, j: (i, ga_cb + j)),
            pl.BlockSpec((tm, tn), lambda i, j: (i, gm_cb + j)),
        ],
        out_specs=pl.BlockSpec((tm, tn), lambda i, j: (i, j)),
        out_shape=jax.ShapeDtypeStruct((n, d), BF16),
        compiler_params=_cparams(("parallel", "arbitrary")),
        name="merge",
    )(y_att, y_ml, wa, wm, proj, proj)


def _out_residual_kernel(x_ref, w_ref, h_ref, o_ref, *, tm, lp, npad, nbatch):
    mix = jnp.dot(x_ref[...], w_ref[...], preferred_element_type=F32)
    valid = _row_valid(pl.program_id(0) * tm, tm, lp, npad, nbatch)
    o_ref[...] = h_ref[...] + jnp.where(valid, mix, 0.0)


def _out_residual(x, w, h, lp, npad, nbatch):
    n, k = x.shape
    d = w.shape[1]
    tm = _divisor_tile(n, 1024, SUBLANE)
    tn = _divisor_tile(d, 1024, LANE)
    return pl.pallas_call(
        functools.partial(_out_residual_kernel, tm=tm, lp=lp, npad=npad, nbatch=nbatch),
        grid=(n // tm, d // tn),
        in_specs=[
            pl.BlockSpec((tm, k), lambda i, j: (i, 0)),
            pl.BlockSpec((k, tn), lambda i, j: (0, j)),
            pl.BlockSpec((tm, tn), lambda i, j: (i, j)),
        ],
        out_specs=pl.BlockSpec((tm, tn), lambda i, j: (i, j)),
        out_shape=jax.ShapeDtypeStruct((n, d), F32),
        compiler_params=_cparams(("parallel", "arbitrary")),
        name="out_residual",
    )(x, w, h)


def _ffn_kernel(x_ref, g_ref, wg_ref, wu_ref, wd_ref, o_ref, xn_ref, *, tm, lp, npad, nbatch):
    f = pl.program_id(1)

    @pl.when(f == 0)
    def _():
        x = x_ref[...]
        ms = jnp.mean(x * x, axis=-1, keepdims=True)
        xn_ref[...] = (x * lax.rsqrt(ms + EPS) * g_ref[...]).astype(BF16)
        o_ref[...] = jnp.zeros_like(o_ref)

    xn = xn_ref[...]
    gate = jnp.dot(xn, wg_ref[...], preferred_element_type=F32)
    up = jnp.dot(xn, wu_ref[...], preferred_element_type=F32)
    act = (gate * _sigmoid(gate) * up).astype(BF16)
    o_ref[...] += jnp.dot(act, wd_ref[...], preferred_element_type=F32)

    @pl.when(f == pl.num_programs(1) - 1)
    def _():
        valid = _row_valid(pl.program_id(0) * tm, tm, lp, npad, nbatch)
        o_ref[...] = x_ref[...] + jnp.where(valid, o_ref[...], 0.0)


def _ffn(h, gain, wg, wu, wd, lp, npad, nbatch):
    n, d = h.shape
    dff = wg.shape[1]
    tm = _divisor_tile(n, 1024, SUBLANE)
    tf = _divisor_tile(dff, 512, LANE)
    return pl.pallas_call(
        functools.partial(_ffn_kernel, tm=tm, lp=lp, npad=npad, nbatch=nbatch),
        grid=(n // tm, dff // tf),
        in_specs=[
            pl.BlockSpec((tm, d), lambda i, f: (i, 0)),
            pl.BlockSpec((1, d), lambda i, f: (0, 0)),
            pl.BlockSpec((d, tf), lambda i, f: (0, f)),
            pl.BlockSpec((d, tf), lambda i, f: (0, f)),
            pl.BlockSpec((tf, d), lambda i, f: (f, 0)),
        ],
        out_specs=pl.BlockSpec((tm, d), lambda i, f: (i, 0)),
        out_shape=jax.ShapeDtypeStruct((n, d), F32),
        scratch_shapes=[pltpu.VMEM((tm, d), BF16)],
        compiler_params=_cparams(("parallel", "arbitrary")),
        name="ffn",
    )(h, gain, wg, wu, wd)


def _bf16_bits(x):
    bits = lax.bitcast_convert_type(x, jnp.uint32)
    rounded = bits + jnp.uint32(0x7FFF) + ((bits >> 16) & jnp.uint32(1))
    return rounded >> 16


def _router_kernel(x_ref, g_ref, rw_ref, xp_ref, route_ref):
    x = x_ref[...]
    d = x.shape[1]
    ms = jnp.mean(x * x, axis=-1, keepdims=True)
    xn = x * lax.rsqrt(ms + EPS) * g_ref[...]
    xp_ref[...] = _bf16_bits(xn[:, :d // 2]) | (_bf16_bits(xn[:, d // 2:]) << 16)

    logits = _dot_f32_3pass(xn, rw_ref[...])
    lane = lax.broadcasted_iota(jnp.int32, logits.shape, 1)
    logits = jnp.where(lane < N_EXPERTS, logits, -jnp.inf)
    v1 = jnp.max(logits, axis=-1, keepdims=True)
    i1 = jnp.min(jnp.where(logits == v1, lane, LANE), axis=-1, keepdims=True)
    rest = jnp.where(lane == i1, -jnp.inf, logits)
    v2 = jnp.max(rest, axis=-1, keepdims=True)
    i2 = jnp.min(jnp.where(rest == v2, lane, LANE), axis=-1, keepdims=True)
    e = jnp.exp(v2 - v1)
    g1 = 1.0 / (1.0 + e)
    g2 = e / (1.0 + e)
    route = jnp.where(lane == 0, i1.astype(F32),
                      jnp.where(lane == 1, i2.astype(F32),
                                jnp.where(lane == 2, g1, jnp.where(lane == 3, g2, 0.0))))
    route_ref[...] = route


def _router(h, gain, rw):
    n, d = h.shape
    tm = _divisor_tile(n, 1024, SUBLANE)
    return pl.pallas_call(
        _router_kernel,
        grid=(n // tm,),
        in_specs=[
            pl.BlockSpec((tm, d), lambda i: (i, 0)),
            pl.BlockSpec((1, d), lambda i: (0, 0)),
            pl.BlockSpec((d, LANE), lambda i: (0, 0)),
        ],
        out_specs=[
            pl.BlockSpec((tm, d // 2), lambda i: (i, 0)),
            pl.BlockSpec((tm, LANE), lambda i: (i, 0)),
        ],
        out_shape=[jax.ShapeDtypeStruct((n, d // 2), jnp.uint32), jax.ShapeDtypeStruct((n, LANE), F32)],
        compiler_params=_cparams(("parallel",)),
        name="router",
    )(h, gain, rw)


GATHER_ROWS = 512


def _row_copy(src_hbm, src_row, dst_ref, dst_row, sem):
    return pltpu.make_async_copy(src_hbm.at[pl.ds(src_row, 1)], dst_ref.at[pl.ds(dst_row, 1)], sem)


def _gather_kernel(idx_ref, x_hbm, o_ref, sem, *, rows):
    def start(r2, carry):
        for u in range(2):
            r = 2 * r2 + u
            _row_copy(x_hbm, idx_ref[0, 0, r], o_ref, r, sem).start(priority=u)
        return carry

    lax.fori_loop(0, rows // 2, start, 0, unroll=4)
    pltpu.make_async_copy(x_hbm.at[pl.ds(0, rows)], o_ref, sem).wait()


def _moe_gather(xp, slot_tok, rows):
    p = slot_tok.shape[0]
    w = xp.shape[1]
    nb = p // rows
    idx3 = slot_tok.reshape(nb, 1, rows)
    return pl.pallas_call(
        functools.partial(_gather_kernel, rows=rows),
        grid=(nb,),
        in_specs=[
            pl.BlockSpec((1, 1, rows), lambda i: (i, 0, 0), memory_space=pltpu.SMEM),
            pl.BlockSpec(memory_space=pl.ANY),
        ],
        out_specs=pl.BlockSpec((rows, w), lambda i: (i, 0)),
        out_shape=jax.ShapeDtypeStruct((p, w), xp.dtype),
        scratch_shapes=[pltpu.SemaphoreType.DMA(())],
        compiler_params=_cparams(("arbitrary",)),
        name="moe_gather",
    )(idx3, xp)


def _moe_ffn_kernel(blk_e_ref, nblk_ref, xg_ref, wg_ref, wu_ref, wd_ref, o_ref, xn_ref):
    i = pl.program_id(0)
    f = pl.program_id(1)

    @pl.when(i < nblk_ref[0])
    def _():
        @pl.when(f == 0)
        def _():
            packed = xg_ref[...]
            half = packed.shape[1]
            lo = lax.bitcast_convert_type(packed << 16, F32)
            hi = lax.bitcast_convert_type(packed & jnp.uint32(0xFFFF0000), F32)
            xn_ref[:, :half] = lo.astype(BF16)
            xn_ref[:, half:] = hi.astype(BF16)
            o_ref[...] = jnp.zeros_like(o_ref)

        xn = xn_ref[...]
        gate = jnp.dot(xn, wg_ref[0], preferred_element_type=F32)
        up = jnp.dot(xn, wu_ref[0], preferred_element_type=F32)
        act = (gate * _sigmoid(gate) * up).astype(BF16)
        o_ref[...] += jnp.dot(act, wd_ref[0], preferred_element_type=F32)

    @pl.when((i >= nblk_ref[0]) & (f == 0))
    def _():
        o_ref[...] = jnp.zeros_like(o_ref)


def _moe_ffn(xg, blk_e, nblk, wg, wu, wd, tm):
    p, half = xg.shape
    d = 2 * half
    dff = wg.shape[2]
    tf = _divisor_tile(dff, 512, LANE)
    nf = dff // tf
    nb = p // tm

    def row_map(i, f, be, nu):
        return (jnp.minimum(i, nu[0] - 1), 0)

    def f_eff(i, f, nu):
        return jnp.where(i < nu[0], f, nf - 1)

    grid_spec = pltpu.PrefetchScalarGridSpec(
        num_scalar_prefetch=2,
        grid=(nb, nf),
        in_specs=[
            pl.BlockSpec((tm, half), row_map),
            pl.BlockSpec((1, d, tf), lambda i, f, be, nu: (be[i], 0, f_eff(i, f, nu))),
            pl.BlockSpec((1, d, tf), lambda i, f, be, nu: (be[i], 0, f_eff(i, f, nu))),
            pl.BlockSpec((1, tf, d), lambda i, f, be, nu: (be[i], f_eff(i, f, nu), 0)),
        ],
        out_specs=pl.BlockSpec((tm, d), lambda i, f, be, nu: (i, 0)),
        scratch_shapes=[pltpu.VMEM((tm, d), BF16)],
    )
    return pl.pallas_call(
        _moe_ffn_kernel,
        grid_spec=grid_spec,
        out_shape=jax.ShapeDtypeStruct((p, d), F32),
        compiler_params=_cparams(("arbitrary", "arbitrary")),
        name="moe_ffn",
    )(blk_e, nblk, xg, wg, wu, wd)


def _combine_kernel(d_ref, route_ref, h_ref, y_hbm, o_ref, y0_ref, y1_ref, sem, *, rows, lp, npad, nbatch):
    def start(r, carry):
        _row_copy(y_hbm, d_ref[0, 0, 2 * r], y0_ref, r, sem).start(priority=0)
        _row_copy(y_hbm, d_ref[0, 0, 2 * r + 1], y1_ref, r, sem).start(priority=1)
        return carry

    lax.fori_loop(0, rows, start, 0, unroll=4)
    pltpu.make_async_copy(y_hbm.at[pl.ds(0, rows)], y0_ref, sem).wait()
    pltpu.make_async_copy(y_hbm.at[pl.ds(0, rows)], y1_ref, sem).wait()

    g0 = route_ref[:, 2:3]
    g1 = route_ref[:, 3:4]
    y = g0 * y0_ref[...] + g1 * y1_ref[...]
    valid = _row_valid(pl.program_id(0) * rows, rows, lp, npad, nbatch)
    o_ref[...] = h_ref[...] + jnp.where(valid, y, 0.0)


def _moe_combine(h, route, dest, yg, rows, lp, npad, nbatch, drop_lead):
    n, d = h.shape
    nb = n // rows
    d3 = dest.reshape(nb, 1, 2 * rows)
    tpb = lp // rows
    if drop_lead:
        out_rows = n - nbatch * rows

        def out_map(i):
            return ((i // tpb) * (tpb - 1) + jnp.maximum(i % tpb - 1, 0), 0)
    else:
        out_rows = n

        def out_map(i):
            return (i, 0)

    return pl.pallas_call(
        functools.partial(_combine_kernel, rows=rows, lp=lp, npad=npad, nbatch=nbatch),
        grid=(nb,),
        in_specs=[
            pl.BlockSpec((1, 1, 2 * rows), lambda i: (i, 0, 0), memory_space=pltpu.SMEM),
            pl.BlockSpec((rows, LANE), lambda i: (i, 0)),
            pl.BlockSpec((rows, d), lambda i: (i, 0)),
            pl.BlockSpec(memory_space=pl.ANY),
        ],
        out_specs=pl.BlockSpec((rows, d), out_map),
        out_shape=jax.ShapeDtypeStruct((out_rows, d), F32),
        scratch_shapes=[pltpu.VMEM((rows, d), F32), pltpu.VMEM((rows, d), F32), pltpu.SemaphoreType.DMA(())],
        compiler_params=_cparams(("arbitrary",)),
        name="moe_combine",
    )(d3, route, h, yg)


def _moe_layout(experts, tm, nblocks):
    n = experts.shape[0]
    flat_e = experts.reshape(-1)
    onehot = (flat_e[:, None] == jnp.arange(N_EXPERTS, dtype=jnp.int32)[None, :]).astype(jnp.int32)
    csum = jnp.cumsum(onehot, axis=0)
    pos = jnp.take_along_axis(csum, flat_e[:, None], axis=1)[:, 0] - 1
    counts = csum[-1]
    padded = (counts + tm - 1) // tm * tm
    ends = jnp.cumsum(padded)
    gstart = ends - padded
    dest = (gstart[flat_e] + pos).astype(jnp.int32)
    flat_tok = jnp.arange(2 * n, dtype=jnp.int32) // 2
    slot_tok = jnp.zeros((nblocks * tm,), jnp.int32).at[dest].set(flat_tok)
    nblk = (ends[-1] // tm).astype(jnp.int32)
    blk = jnp.minimum(jnp.arange(nblocks, dtype=jnp.int32), nblk - 1)
    blk_e = jnp.minimum(jnp.searchsorted(ends, blk * tm, side="right"), N_EXPERTS - 1).astype(jnp.int32)
    return dest, slot_tok, blk_e, nblk.reshape(1)


def _moe_block(h, gain, rw, wg, wu, wd, lp, npad, nbatch, lead_rows):
    n, d = h.shape
    xp, route = _router(h, gain, rw)
    experts = route[:, :2].astype(jnp.int32)
    tm = _divisor_tile(n, 1024, GATHER_ROWS) if n % GATHER_ROWS == 0 else n
    rows = min(GATHER_ROWS, tm)
    nblocks = -(-(2 * n + N_EXPERTS * (tm - 1)) // tm)
    dest, slot_tok, blk_e, nblk = _moe_layout(experts, tm, nblocks)
    xg = _moe_gather(xp, slot_tok, rows)
    yg = _moe_ffn(xg, blk_e, nblk, wg, wu, wd, tm)
    return _moe_combine(h, route, dest, yg, rows, lp, npad, nbatch, drop_lead=(lead_rows == rows))


def _inproj_layout(d_model):
    att_w = HEADS * ATT_DH
    qk_w = HEADS * ML_DK
    v_w = HEADS * ML_DV
    names = ("qa", "ka", "va", "fa", "qm", "km", "vm", "im", "fm", "om", "ga", "gm")
    widths = (att_w, att_w, att_w, HEADS, qk_w, qk_w, v_w, HEADS, HEADS, v_w, d_model, d_model)
    src, off = {}, 0
    for nm, wd in zip(names, widths):
        src[nm] = (off, wd)
        off += wd
    big_order = ("vm", "om", "ga", "gm", "qa", "ka", "va", "qm", "km")
    cols, off = {}, 0
    for nm in big_order:
        cols[nm] = off
        off += src[nm][1]
    return src, big_order, cols


def kernel(x, meta_tokens, ln_mix, w_in, att_q_norm, att_k_norm, att_f_bias, ml_conv_w, ml_conv_b, ml_i_bias,
           ml_f_bias, ml_h_norm, w_branch_att, w_branch_ml, w_out, ln_ffn, dense_w_gate, dense_w_up, dense_w_down,
           router_w, moe_w_gate, moe_w_up, moe_w_down):
    nbatch, seq, d = x.shape
    depth = w_in.shape[0]
    lp = -(-(CHUNK + seq) // SEQ_TILE) * SEQ_TILE
    npad = lp - seq - N_META
    n = nbatch * lp
    src, big_order, cols = _inproj_layout(d)

    h = jnp.concatenate([
        jnp.zeros((nbatch, npad, d), x.dtype),
        jnp.broadcast_to(meta_tokens[None].astype(x.dtype), (nbatch, N_META, d)),
        x], axis=1).reshape(n, d)

    for layer in range(depth):
        wl = w_in[layer]
        w_big = jnp.concatenate([wl[:, src[nm][0]:src[nm][0] + src[nm][1]] for nm in big_order], axis=1).astype(BF16)
        w_gates = jnp.concatenate(
            [wl[:, src[nm][0]:src[nm][0] + HEADS] for nm in ("fa", "fm", "im")]
            + [jnp.zeros((d, LANE - 3 * HEADS), F32)], axis=1)
        gate_bias = jnp.concatenate([att_f_bias[layer], ml_f_bias[layer], ml_i_bias[layer],
                                     jnp.zeros((LANE - 3 * HEADS,), F32)]).reshape(1, LANE)

        proj, gates = _norm_inproj(h, ln_mix[layer].reshape(1, d), w_big, w_gates)
        proj3 = proj.reshape(nbatch, lp, -1)
        gcol, grow = _gate_prep(gates.reshape(nbatch, lp, LANE), gate_bias, npad)
        qa, ka, qm, kmt = _qk_prep(proj3, cols, att_q_norm[layer].reshape(1, ATT_DH),
                                   att_k_norm[layer].reshape(1, ATT_DH), ml_conv_w[layer],
                                   ml_conv_b[layer].reshape(1, -1), npad)
        c_rows = grow[:, :HEADS, :].reshape(nbatch * HEADS, lp // SEQ_TILE, 1, SEQ_TILE)
        y_att = _fox_attention(qa, ka, proj3, cols["va"], c_rows, npad)
        y_ml = _mlstm(qm, kmt, proj3, cols, gcol, grow, ml_h_norm[layer].reshape(1, -1))
        merged = _merge(y_att.reshape(n, -1), y_ml.reshape(n, -1), w_branch_att[layer].astype(BF16),
                        w_branch_ml[layer].astype(BF16), proj, cols)
        h = _out_residual(merged, w_out[layer].astype(BF16), h, lp, npad, nbatch)

        j = layer // 2
        gain = ln_ffn[layer].reshape(1, d)
        if layer % 2 == 0:
            h = _ffn(h, gain, dense_w_gate[j].astype(BF16), dense_w_up[j].astype(BF16),
                     dense_w_down[j].astype(BF16), lp, npad, nbatch)
        else:
            rw = jnp.concatenate([router_w[j], jnp.zeros((d, LANE - N_EXPERTS), F32)], axis=1)
            h = _moe_block(h, gain, rw, moe_w_gate[j].astype(BF16), moe_w_up[j].astype(BF16),
                           moe_w_down[j].astype(BF16), lp, npad, nbatch,
                           lead_rows=(lp - seq) if layer == depth - 1 else 0)

    if h.shape[0] == nbatch * seq:
        return h.reshape(nbatch, seq, d)
    return h.reshape(nbatch, lp, d)[:, lp - seq:, :]
```

```python
import functools

import jax
import jax.numpy as jnp
from jax import lax
from jax.experimental import pallas as pl
from jax.experimental.pallas import tpu as pltpu

N_META = 16
CHUNK = 128
HEADS = 8
ATT_DH = 128
ML_DK = 128
ML_DV = 256
CONV_WIDTH = 4
N_EXPERTS = 8
EPS = 1e-6
NEG = -1e30
LOG2E = 1.4426950408889634

LANE = 128
SUBLANE = 8
VMEM_BYTES_V7X = 64 * 1024 * 1024
VMEM_LIMIT = 56 * 1024 * 1024

SEQ_TILE = 512
HALO = SUBLANE

F32 = jnp.float32
BF16 = jnp.bfloat16
HIGHEST = lax.Precision.HIGHEST


def _divisor_tile(n, pref, align):
    if n <= pref:
        return n
    t = (pref // align) * align
    while t >= align:
        if n % t == 0:
            return t
        t -= align
    return n


def _cparams(sem):
    return pltpu.CompilerParams(dimension_semantics=sem, vmem_limit_bytes=VMEM_LIMIT)


def _row_valid(row0, nrows, lp, npad, nbatch):
    r = row0 + lax.broadcasted_iota(jnp.int32, (nrows, 1), 0)
    invalid = (r >= 0) & (r < npad)
    for b in range(1, nbatch):
        invalid = invalid | ((r >= b * lp) & (r < b * lp + npad))
    return jnp.logical_not(invalid)


def _log_sigmoid(x):
    return jnp.minimum(x, 0.0) - jnp.log1p(jnp.exp(-jnp.abs(x)))


def _sigmoid(x):
    return 1.0 / (1.0 + jnp.exp(-x))


def _dot_f32_3pass(x, w):
    xh = x.astype(BF16)
    xl = (x - xh.astype(F32)).astype(BF16)
    wh = w.astype(BF16)
    wl = (w - wh.astype(F32)).astype(BF16)
    return (jnp.dot(xh, wh, preferred_element_type=F32) + jnp.dot(xl, wh, preferred_element_type=F32)
            + jnp.dot(xh, wl, preferred_element_type=F32))


def _norm_inproj_kernel(x_ref, g_ref, w_ref, wg_ref, o_ref, og_ref, xn_ref):
    @pl.when(pl.program_id(1) == 0)
    def _():
        x = x_ref[...]
        ms = jnp.mean(x * x, axis=-1, keepdims=True)
        xn = x * lax.rsqrt(ms + EPS) * g_ref[...]
        xn_ref[...] = xn.astype(BF16)
        og_ref[...] = _dot_f32_3pass(xn, wg_ref[...])

    o_ref[...] = jnp.dot(xn_ref[...], w_ref[...], preferred_element_type=F32).astype(o_ref.dtype)


def _norm_inproj(h, gain, w_big, w_gates):
    n, d = h.shape
    wn = w_big.shape[1]
    tm = _divisor_tile(n, 1024, SUBLANE)
    tn = _divisor_tile(wn, 1024, LANE)
    return pl.pallas_call(
        _norm_inproj_kernel,
        grid=(n // tm, wn // tn),
        in_specs=[
            pl.BlockSpec((tm, d), lambda i, j: (i, 0)),
            pl.BlockSpec((1, d), lambda i, j: (0, 0)),
            pl.BlockSpec((d, tn), lambda i, j: (0, j)),
            pl.BlockSpec((d, LANE), lambda i, j: (0, 0)),
        ],
        out_specs=[
            pl.BlockSpec((tm, tn), lambda i, j: (i, j)),
            pl.BlockSpec((tm, LANE), lambda i, j: (i, 0)),
        ],
        out_shape=[jax.ShapeDtypeStruct((n, wn), BF16), jax.ShapeDtypeStruct((n, LANE), F32)],
        scratch_shapes=[pltpu.VMEM((tm, d), BF16)],
        compiler_params=_cparams(("parallel", "arbitrary")),
        name="norm_inproj",
    )(h, gain, w_big, w_gates)


def _gate_prep_kernel(g_ref, b_ref, col_ref, row_ref, carry_ref, *, npad):
    c = pl.program_id(1)

    @pl.when(c == 0)
    def _():
        carry_ref[...] = jnp.zeros_like(carry_ref)

    lane = lax.broadcasted_iota(jnp.int32, (CHUNK, LANE), 1)
    r_i = lax.broadcasted_iota(jnp.int32, (CHUNK, CHUNK), 0)
    c_i = lax.broadcasted_iota(jnp.int32, (CHUNK, CHUNK), 1)
    tril = (r_i >= c_i).astype(F32)
    carry = carry_ref[0:1, :]
    for u in range(g_ref.shape[1] // CHUNK):
        rows = slice(u * CHUNK, (u + 1) * CHUNK)
        x = g_ref[0, rows, :] + b_ref[...]
        pos = c * g_ref.shape[1] + u * CHUNK + lax.broadcasted_iota(jnp.int32, (CHUNK, LANE), 0)
        valid = pos >= npad
        ls = _log_sigmoid(x)
        z = jnp.where(lane < HEADS, ls, jnp.where((lane < 2 * HEADS) & valid, ls, 0.0))
        s = jnp.dot(tril, z, preferred_element_type=F32, precision=HIGHEST)
        s = s + jnp.where(lane < HEADS, carry, 0.0)
        carry = s[CHUNK - 1:CHUNK, :]
        ipre = jnp.where(valid, x, -jnp.inf)
        col = jnp.where(lane < HEADS, s * LOG2E,
                        jnp.where(lane < 2 * HEADS, s, jnp.where(lane < 3 * HEADS, ipre, 0.0)))
        col_ref[0, rows, :] = col
        row_ref[0, :, rows] = col.T[:3 * HEADS, :]
    carry_ref[...] = jnp.broadcast_to(carry, carry_ref.shape)


def _gate_prep(gates3, bias, npad):
    b, lp, _ = gates3.shape
    rows = SEQ_TILE
    nc = lp // rows
    return pl.pallas_call(
        functools.partial(_gate_prep_kernel, npad=npad),
        grid=(b, nc),
        in_specs=[
            pl.BlockSpec((1, rows, LANE), lambda bi, c: (bi, c, 0)),
            pl.BlockSpec((1, LANE), lambda bi, c: (0, 0)),
        ],
        out_specs=[
            pl.BlockSpec((1, rows, LANE), lambda bi, c: (bi, c, 0)),
            pl.BlockSpec((1, 3 * HEADS, rows), lambda bi, c: (bi, 0, c)),
        ],
        out_shape=[jax.ShapeDtypeStruct((b, lp, LANE), F32), jax.ShapeDtypeStruct((b, 3 * HEADS, lp), F32)],
        scratch_shapes=[pltpu.VMEM((SUBLANE, LANE), F32)],
        compiler_params=_cparams(("parallel", "arbitrary")),
        name="gate_prep",
    )(gates3, bias)


def _qk_prep_kernel(qa_ref, ka_ref, qm_ref, km_ref, qmh_ref, kmh_ref, gq_ref, gk_ref, cw_ref, cb_ref,
                    qa_o, ka_o, qm_o, kmt_o, win_ref, *, npad):
    c = pl.program_id(1)
    ml_w = HEADS * ML_DK

    for hh in range(HEADS):
        sl = slice(hh * ATT_DH, (hh + 1) * ATT_DH)
        q = qa_ref[0, :, sl].astype(F32)
        q = q * lax.rsqrt(jnp.mean(q * q, axis=-1, keepdims=True) + EPS) * gq_ref[...]
        qa_o[0, :, sl] = (q * (ATT_DH ** -0.5 * LOG2E)).astype(BF16)
        k = ka_ref[0, :, sl].astype(F32)
        k = k * lax.rsqrt(jnp.mean(k * k, axis=-1, keepdims=True) + EPS) * gk_ref[...]
        ka_o[0, :, sl] = k.astype(BF16)

    pos = c * CHUNK + lax.broadcasted_iota(jnp.int32, (CHUNK, 1), 0)
    hpos = c * CHUNK - HALO + lax.broadcasted_iota(jnp.int32, (HALO, 1), 0)
    row_ok = pos >= npad
    halo_ok = (hpos >= npad) & (c > 0)

    def conv_silu(x_ref, xh_ref, col0):
        win_ref[0:HALO, :] = jnp.where(halo_ok, xh_ref[0].astype(F32), 0.0)
        win_ref[HALO:HALO + CHUNK, :] = jnp.where(row_ok, x_ref[0].astype(F32), 0.0)
        y = jnp.broadcast_to(cb_ref[:, col0:col0 + ml_w], (CHUNK, ml_w))
        for j in range(CONV_WIDTH):
            off = HALO - (CONV_WIDTH - 1) + j
            y = y + cw_ref[j:j + 1, col0:col0 + ml_w] * win_ref[off:off + CHUNK, :]
        return y * _sigmoid(y)

    yq = conv_silu(qm_ref, qmh_ref, 0)
    qm_o[0] = yq.astype(BF16)
    yk = conv_silu(km_ref, kmh_ref, ml_w) * (ML_DK ** -0.5)
    for hh in range(HEADS):
        sl = slice(hh * ML_DK, (hh + 1) * ML_DK)
        kmt_o[0, sl, :] = yk[:, sl].T.astype(BF16)


def _qk_prep(proj3, cols, gq, gk, conv_w, conv_b, npad):
    b, lp, _ = proj3.shape
    nc = lp // CHUNK
    w = HEADS * ATT_DH
    qa_c, ka_c, qm_c, km_c = (cols[k] // w for k in ("qa", "ka", "qm", "km"))
    hpc = CHUNK // HALO

    def main(cb):
        return pl.BlockSpec((1, CHUNK, w), lambda bi, c: (bi, c, cb))

    def halo(cb):
        return pl.BlockSpec((1, HALO, w), lambda bi, c: (bi, jnp.maximum(c * hpc - 1, 0), cb))

    full = lambda shape: pl.BlockSpec(shape, lambda bi, c: (0,) * len(shape))
    return pl.pallas_call(
        functools.partial(_qk_prep_kernel, npad=npad),
        grid=(b, nc),
        in_specs=[main(qa_c), main(ka_c), main(qm_c), main(km_c), halo(qm_c), halo(km_c),
                  full((1, ATT_DH)), full((1, ATT_DH)), full((CONV_WIDTH, 2 * w)), full((1, 2 * w))],
        out_specs=[
            pl.BlockSpec((1, CHUNK, w), lambda bi, c: (bi, c, 0)),
            pl.BlockSpec((1, CHUNK, w), lambda bi, c: (bi, c, 0)),
            pl.BlockSpec((1, CHUNK, w), lambda bi, c: (bi, c, 0)),
            pl.BlockSpec((1, w, CHUNK), lambda bi, c: (bi, 0, c)),
        ],
        out_shape=[jax.ShapeDtypeStruct((b, lp, w), BF16)] * 3 + [jax.ShapeDtypeStruct((b, w, lp), BF16)],
        scratch_shapes=[pltpu.VMEM((HALO + CHUNK, w), F32)],
        compiler_params=_cparams(("parallel", "arbitrary")),
        name="qk_prep",
    )(proj3, proj3, proj3, proj3, proj3, proj3, gq, gk, conv_w, conv_b)


FOX_PART_ROWS = 512
FOX_Q_TILE_MAX = 1536


def _fox_kernel(q_ref, k_ref, v_ref, c_ref, o_ref, m_ref, acc_ref, s0_ref, s1_ref, *, tq, tk, npad):
    i = pl.program_id(2)
    r = tq // tk
    hq = min(FOX_PART_ROWS, tk)
    slots = (s0_ref, s1_ref)
    c0 = c_ref[0, r * i][:, 0:1]
    m_ref[...] = jnp.full_like(m_ref, NEG)
    acc_ref[...] = jnp.zeros_like(acc_ref)
    ones = jnp.ones((tk, LANE), BF16)

    def kv_start(j):
        return j * tk if isinstance(j, int) else pl.multiple_of(j * tk, tk)

    def scores(j, s_ref, masked, row0):
        k = k_ref[0, pl.ds(kv_start(j), tk), :]
        s = lax.dot_general(q_ref[0, row0:, :], k, (((1,), (1,)), ((), ())), preferred_element_type=F32)
        s = s + (c0 - c_ref[0, j])
        if masked:
            qpos = i * tq + row0 + lax.broadcasted_iota(jnp.int32, (tq - row0, tk), 0)
            kpos = j * tk + lax.broadcasted_iota(jnp.int32, (tq - row0, tk), 1)
            s = jnp.where((kpos <= qpos) & (kpos >= npad), s, NEG)
        s_ref[row0:, :] = s

    def accumulate(j, s_ref, row0):
        vext = jnp.concatenate([v_ref[0, pl.ds(kv_start(j), tk), :], ones], axis=1)
        for part in range(row0 // hq, tq // hq):
            rows = slice(part * hq, (part + 1) * hq)
            s = s_ref[rows, :]
            m_prev = m_ref[rows, :]
            m_new = jnp.maximum(m_prev, jnp.broadcast_to(jnp.max(s, axis=1, keepdims=True), (hq, LANE)))
            alpha = jnp.exp2(m_prev - m_new)
            p = jnp.exp2(s - jnp.concatenate([m_new] * (tk // LANE), axis=1))
            pv = jnp.dot(p.astype(BF16), vext, preferred_element_type=F32)
            acc_ref[rows, :] = jnp.concatenate([alpha, alpha], axis=1) * acc_ref[rows, :] + pv
            m_ref[rows, :] = m_new

    def run(chunks):
        for n, (j, d) in enumerate(chunks):
            if n + 1 < len(chunks):
                jn, dn = chunks[n + 1]
                scores(jn, slots[(n + 1) % 2], dn is not None, 0 if dn is None else dn * tk)
            accumulate(j, slots[n % 2], 0 if d is None else d * tk)

    first = r * i
    diag = [(first + d, d) for d in range(r)]
    scores(0, s0_ref, True, 0)

    n_plain = jnp.maximum(first - 1, 0)
    odd = lax.bitwise_and(n_plain, 1)

    def pair_body(p, carry):
        j = 2 * p
        scores(j + 1, s1_ref, False, 0)
        accumulate(j, s0_ref, 0)
        scores(j + 2, s0_ref, False, 0)
        accumulate(j + 1, s1_ref, 0)
        return carry

    lax.fori_loop(0, lax.shift_right_logical(n_plain, 1), pair_body, 0)

    @pl.when(i == 0)
    def _():
        run(diag)

    @pl.when((i > 0) & (odd == 0))
    def _():
        run([(first - 1, None)] + diag)

    @pl.when((i > 0) & (odd == 1))
    def _():
        run([(first - 2, None), (first - 1, None)] + diag)

    o_ref[0] = (acc_ref[:, :ATT_DH] / acc_ref[:, ATT_DH:]).astype(o_ref.dtype)


def _fox_attention(qa, ka, proj3, v_col, c_rows, npad):
    b, lp, w = qa.shape
    tk = SEQ_TILE
    tq = _divisor_tile(lp, FOX_Q_TILE_MAX, tk)
    nq = lp // tq
    nk = lp // tk
    vcb = v_col // ATT_DH
    return pl.pallas_call(
        functools.partial(_fox_kernel, tq=tq, tk=tk, npad=npad),
        grid=(b, HEADS, nq),
        in_specs=[
            pl.BlockSpec((1, tq, ATT_DH), lambda bi, h, i: (bi, i, h)),
            pl.BlockSpec((1, lp, ATT_DH), lambda bi, h, i: (bi, 0, h)),
            pl.BlockSpec((1, lp, ATT_DH), lambda bi, h, i: (bi, 0, vcb + h)),
            pl.BlockSpec((1, nk, 1, tk), lambda bi, h, i: (bi * HEADS + h, 0, 0, 0)),
        ],
        out_specs=pl.BlockSpec((1, tq, ATT_DH), lambda bi, h, i: (bi, i, h)),
        out_shape=jax.ShapeDtypeStruct((b, lp, w), BF16),
        scratch_shapes=[pltpu.VMEM((tq, LANE), F32), pltpu.VMEM((tq, ATT_DH + LANE), F32),
                        pltpu.VMEM((tq, tk), F32), pltpu.VMEM((tq, tk), F32)],
        compiler_params=_cparams(("parallel", "parallel", "arbitrary")),
        name="fox_attention",
    )(qa, ka, proj3, c_rows)


def _mlstm_kernel(q_ref, kt_ref, v_ref, om_ref, col_ref, row_ref, hn_ref, o_ref, ct_ref, m_ref):
    c = pl.program_id(1)

    @pl.when(c == 0)
    def _():
        ct_ref[...] = jnp.zeros_like(ct_ref)
        m_ref[...] = jnp.zeros_like(m_ref)

    t = CHUNK
    r_i = lax.broadcasted_iota(jnp.int32, (t, t), 0)
    c_i = lax.broadcasted_iota(jnp.int32, (t, t), 1)
    causal = r_i >= c_i
    ones = jnp.ones((t, LANE), BF16)

    for hh in range(HEADS):
        q = q_ref[0, :, hh * ML_DK:(hh + 1) * ML_DK]
        kt = kt_ref[0, hh * ML_DK:(hh + 1) * ML_DK, :]
        v = v_ref[0, :, hh * ML_DV:(hh + 1) * ML_DV]
        vext = jnp.concatenate([v, ones], axis=1)
        bt_c = col_ref[0, :, HEADS + hh:HEADS + hh + 1]
        bt_r = row_ref[0, HEADS + hh:HEADS + hh + 1, :]
        it_r = row_ref[0, 2 * HEADS + hh:2 * HEADS + hh + 1, :]
        g = bt_r[:, t - 1:t]
        m = m_ref[hh, 0:1, 0:1]

        dmat = jnp.where(causal, bt_c - bt_r + it_r, -jnp.inf)
        inter = bt_c + m
        m_row = jnp.maximum(inter, jnp.max(dmat, axis=1, keepdims=True))
        w_intra = jnp.exp(dmat - m_row)
        w_inter = jnp.exp(inter - m_row)
        qk = jnp.dot(q, kt, preferred_element_type=F32)
        a = (qk * w_intra).astype(BF16)
        ct = ct_ref[hh]
        nd = (jnp.dot(a, vext, preferred_element_type=F32)
              + w_inter * jnp.dot(q, ct.astype(BF16), preferred_element_type=F32))
        num = nd[:, :ML_DV]
        den = nd[:, ML_DV:ML_DV + 1]
        denom = jnp.maximum(jnp.abs(den), jnp.exp(-m_row))
        hv = num / denom

        src = g - bt_r + it_r
        m_new = jnp.maximum(g + m, jnp.max(src, axis=1, keepdims=True))
        w_src = jnp.exp(src - m_new)
        decay = jnp.exp(g + m - m_new)
        ks = (kt.astype(F32) * w_src).astype(BF16)
        ct_ref[hh] = decay * ct + jnp.dot(ks, vext, preferred_element_type=F32)
        m_ref[hh] = jnp.broadcast_to(m_new, (SUBLANE, LANE))

        hv = hv * lax.rsqrt(jnp.mean(hv * hv, axis=-1, keepdims=True) + EPS)
        hv = hv * hn_ref[:, hh * ML_DV:(hh + 1) * ML_DV]
        og = _sigmoid(om_ref[0, :, hh * ML_DV:(hh + 1) * ML_DV].astype(F32))
        o_ref[0, :, hh * ML_DV:(hh + 1) * ML_DV] = (og * hv).astype(o_ref.dtype)


def _mlstm(qm, kmt, proj3, cols, gcol, grow, hnorm):
    b, lp, qw = qm.shape
    nc = lp // CHUNK
    vw = HEADS * ML_DV
    v_cb = cols["vm"] // vw
    om_cb = cols["om"] // vw
    return pl.pallas_call(
        _mlstm_kernel,
        grid=(b, nc),
        in_specs=[
            pl.BlockSpec((1, CHUNK, qw), lambda bi, c: (bi, c, 0)),
            pl.BlockSpec((1, qw, CHUNK), lambda bi, c: (bi, 0, c)),
            pl.BlockSpec((1, CHUNK, vw), lambda bi, c: (bi, c, v_cb)),
            pl.BlockSpec((1, CHUNK, vw), lambda bi, c: (bi, c, om_cb)),
            pl.BlockSpec((1, CHUNK, LANE), lambda bi, c: (bi, c, 0)),
            pl.BlockSpec((1, 3 * HEADS, CHUNK), lambda bi, c: (bi, 0, c)),
            pl.BlockSpec((1, vw), lambda bi, c: (0, 0)),
        ],
        out_specs=pl.BlockSpec((1, CHUNK, vw), lambda bi, c: (bi, c, 0)),
        out_shape=jax.ShapeDtypeStruct((b, lp, vw), BF16),
        scratch_shapes=[pltpu.VMEM((HEADS, ML_DK, ML_DV + LANE), F32), pltpu.VMEM((HEADS, SUBLANE, LANE), F32)],
        compiler_params=_cparams(("parallel", "arbitrary")),
        name="mlstm",
    )(qm, kmt, proj3, proj3, gcol, grow, hnorm)


def _merge_kernel(ya_ref, ym_ref, wa_ref, wm_ref, ga_ref, gm_ref, o_ref):
    a = jnp.dot(ya_ref[...], wa_ref[...], preferred_element_type=F32)
    m = jnp.dot(ym_ref[...], wm_ref[...], preferred_element_type=F32)
    out = _sigmoid(ga_ref[...].astype(F32)) * a + _sigmoid(gm_ref[...].astype(F32)) * m
    o_ref[...] = out.astype(o_ref.dtype)


def _merge(y_att, y_ml, wa, wm, proj, cols):
    n, ka = y_att.shape
    km = y_ml.shape[1]
    d = wa.shape[1]
    tm = _divisor_tile(n, 1024, SUBLANE)
    tn = _divisor_tile(d, 1024, LANE)
    ga_cb = cols["ga"] // tn
    gm_cb = cols["gm"] // tn
    return pl.pallas_call(
        _merge_kernel,
        grid=(n // tm, d // tn),
        in_specs=[
            pl.BlockSpec((tm, ka), lambda i, j: (i, 0)),
            pl.BlockSpec((tm, km), lambda i, j: (i, 0)),
            pl.BlockSpec((ka, tn), lambda i, j: (0, j)),
            pl.BlockSpec((km, tn), lambda i, j: (0, j)),
            pl.BlockSpec((tm, tn), lambda i, j: (i, ga_cb + j)),
            pl.BlockSpec((tm, tn), lambda i, j: (i, gm_cb + j)),
        ],
        out_specs=pl.BlockSpec((tm, tn), lambda i, j: (i, j)),
        out_shape=jax.ShapeDtypeStruct((n, d), BF16),
        compiler_params=_cparams(("parallel", "arbitrary")),
        name="merge",
    )(y_att, y_ml, wa, wm, proj, proj)


def _out_residual_kernel(x_ref, w_ref, h_ref, o_ref, *, tm, lp, npad, nbatch):
    mix = jnp.dot(x_ref[...], w_ref[...], preferred_element_type=F32)
    valid = _row_valid(pl.program_id(0) * tm, tm, lp, npad, nbatch)
    o_ref[...] = h_ref[...] + jnp.where(valid, mix, 0.0)


def _out_residual(x, w, h, lp, npad, nbatch):
    n, k = x.shape
    d = w.shape[1]
    tm = _divisor_tile(n, 1024, SUBLANE)
    tn = _divisor_tile(d, 1024, LANE)
    return pl.pallas_call(
        functools.partial(_out_residual_kernel, tm=tm, lp=lp, npad=npad, nbatch=nbatch),
        grid=(n // tm, d // tn),
        in_specs=[
            pl.BlockSpec((tm, k), lambda i, j: (i, 0)),
            pl.BlockSpec((k, tn), lambda i, j: (0, j)),
            pl.BlockSpec((tm, tn), lambda i, j: (i, j)),
        ],
        out_specs=pl.BlockSpec((tm, tn), lambda i, j: (i, j)),
        out_shape=jax.ShapeDtypeStruct((n, d), F32),
        compiler_params=_cparams(("parallel", "arbitrary")),
        name="out_residual",
    )(x, w, h)


def _ffn_kernel(x_ref, g_ref, wg_ref, wu_ref, wd_ref, o_ref, xn_ref, *, tm, lp, npad, nbatch):
    f = pl.program_id(1)

    @pl.when(f == 0)
    def _():
        x = x_ref[...]
        ms = jnp.mean(x * x, axis=-1, keepdims=True)
        xn_ref[...] = (x * lax.rsqrt(ms + EPS) * g_ref[...]).astype(BF16)
        o_ref[...] = jnp.zeros_like(o_ref)

    xn = xn_ref[...]
    gate = jnp.dot(xn, wg_ref[...], preferred_element_type=F32)
    up = jnp.dot(xn, wu_ref[...], preferred_element_type=F32)
    act = (gate * _sigmoid(gate) * up).astype(BF16)
    o_ref[...] += jnp.dot(act, wd_ref[...], preferred_element_type=F32)

    @pl.when(f == pl.num_programs(1) - 1)
    def _():
        valid = _row_valid(pl.program_id(0) * tm, tm, lp, npad, nbatch)
        o_ref[...] = x_ref[...] + jnp.where(valid, o_ref[...], 0.0)


def _ffn(h, gain, wg, wu, wd, lp, npad, nbatch):
    n, d = h.shape
    dff = wg.shape[1]
    tm = _divisor_tile(n, 1024, SUBLANE)
    tf = _divisor_tile(dff, 512, LANE)
    return pl.pallas_call(
        functools.partial(_ffn_kernel, tm=tm, lp=lp, npad=npad, nbatch=nbatch),
        grid=(n // tm, dff // tf),
        in_specs=[
            pl.BlockSpec((tm, d), lambda i, f: (i, 0)),
            pl.BlockSpec((1, d), lambda i, f: (0, 0)),
            pl.BlockSpec((d, tf), lambda i, f: (0, f)),
            pl.BlockSpec((d, tf), lambda i, f: (0, f)),
            pl.BlockSpec((tf, d), lambda i, f: (f, 0)),
        ],
        out_specs=pl.BlockSpec((tm, d), lambda i, f: (i, 0)),
        out_shape=jax.ShapeDtypeStruct((n, d), F32),
        scratch_shapes=[pltpu.VMEM((tm, d), BF16)],
        compiler_params=_cparams(("parallel", "arbitrary")),
        name="ffn",
    )(h, gain, wg, wu, wd)


def _bf16_bits(x):
    bits = lax.bitcast_convert_type(x, jnp.uint32)
    rounded = bits + jnp.uint32(0x7FFF) + ((bits >> 16) & jnp.uint32(1))
    return rounded >> 16


def _router_kernel(x_ref, g_ref, rw_ref, xp_ref, route_ref):
    x = x_ref[...]
    d = x.shape[1]
    ms = jnp.mean(x * x, axis=-1, keepdims=True)
    xn = x * lax.rsqrt(ms + EPS) * g_ref[...]
    xp_ref[...] = _bf16_bits(xn[:, :d // 2]) | (_bf16_bits(xn[:, d // 2:]) << 16)

    logits = _dot_f32_3pass(xn, rw_ref[...])
    lane = lax.broadcasted_iota(jnp.int32, logits.shape, 1)
    logits = jnp.where(lane < N_EXPERTS, logits, -jnp.inf)
    v1 = jnp.max(logits, axis=-1, keepdims=True)
    i1 = jnp.min(jnp.where(logits == v1, lane, LANE), axis=-1, keepdims=True)
    rest = jnp.where(lane == i1, -jnp.inf, logits)
    v2 = jnp.max(rest, axis=-1, keepdims=True)
    i2 = jnp.min(jnp.where(rest == v2, lane, LANE), axis=-1, keepdims=True)
    e = jnp.exp(v2 - v1)
    g1 = 1.0 / (1.0 + e)
    g2 = e / (1.0 + e)
    route = jnp.where(lane == 0, i1.astype(F32),
                      jnp.where(lane == 1, i2.astype(F32),
                                jnp.where(lane == 2, g1, jnp.where(lane == 3, g2, 0.0))))
    route_ref[...] = route


def _router(h, gain, rw):
    n, d = h.shape
    tm = _divisor_tile(n, 1024, SUBLANE)
    return pl.pallas_call(
        _router_kernel,
        grid=(n // tm,),
        in_specs=[
            pl.BlockSpec((tm, d), lambda i: (i, 0)),
            pl.BlockSpec((1, d), lambda i: (0, 0)),
            pl.BlockSpec((d, LANE), lambda i: (0, 0)),
        ],
        out_specs=[
            pl.BlockSpec((tm, d // 2), lambda i: (i, 0)),
            pl.BlockSpec((tm, LANE), lambda i: (i, 0)),
        ],
        out_shape=[jax.ShapeDtypeStruct((n, d // 2), jnp.uint32), jax.ShapeDtypeStruct((n, LANE), F32)],
        compiler_params=_cparams(("parallel",)),
        name="router",
    )(h, gain, rw)


GATHER_ROWS = 512


def _row_copy(src_hbm, src_row, dst_ref, dst_row, sem):
    return pltpu.make_async_copy(src_hbm.at[pl.ds(src_row, 1)], dst_ref.at[pl.ds(dst_row, 1)], sem)


def _gather_kernel(idx_ref, x_hbm, o_ref, sem, *, rows):
    def start(r2, carry):
        for u in range(2):
            r = 2 * r2 + u
            _row_copy(x_hbm, idx_ref[0, 0, r], o_ref, r, sem).start(priority=u)
        return carry

    lax.fori_loop(0, rows // 2, start, 0, unroll=4)
    pltpu.make_async_copy(x_hbm.at[pl.ds(0, rows)], o_ref, sem).wait()


def _moe_gather(xp, slot_tok, rows):
    p = slot_tok.shape[0]
    w = xp.shape[1]
    nb = p // rows
    idx3 = slot_tok.reshape(nb, 1, rows)
    return pl.pallas_call(
        functools.partial(_gather_kernel, rows=rows),
        grid=(nb,),
        in_specs=[
            pl.BlockSpec((1, 1, rows), lambda i: (i, 0, 0), memory_space=pltpu.SMEM),
            pl.BlockSpec(memory_space=pl.ANY),
        ],
        out_specs=pl.BlockSpec((rows, w), lambda i: (i, 0)),
        out_shape=jax.ShapeDtypeStruct((p, w), xp.dtype),
        scratch_shapes=[pltpu.SemaphoreType.DMA(())],
        compiler_params=_cparams(("arbitrary",)),
        name="moe_gather",
    )(idx3, xp)


def _moe_ffn_kernel(blk_e_ref, nblk_ref, xg_ref, wg_ref, wu_ref, wd_ref, o_ref, xn_ref):
    i = pl.program_id(0)
    f = pl.program_id(1)

    @pl.when(i < nblk_ref[0])
    def _():
        @pl.when(f == 0)
        def _():
            packed = xg_ref[...]
            half = packed.shape[1]
            lo = lax.bitcast_convert_type(packed << 16, F32)
            hi = lax.bitcast_convert_type(packed & jnp.uint32(0xFFFF0000), F32)
            xn_ref[:, :half] = lo.astype(BF16)
            xn_ref[:, half:] = hi.astype(BF16)
            o_ref[...] = jnp.zeros_like(o_ref)

        xn = xn_ref[...]
        gate = jnp.dot(xn, wg_ref[0], preferred_element_type=F32)
        up = jnp.dot(xn, wu_ref[0], preferred_element_type=F32)
        act = (gate * _sigmoid(gate) * up).astype(BF16)
        o_ref[...] += jnp.dot(act, wd_ref[0], preferred_element_type=F32)

    @pl.when((i >= nblk_ref[0]) & (f == 0))
    def _():
        o_ref[...] = jnp.zeros_like(o_ref)


def _moe_ffn(xg, blk_e, nblk, wg, wu, wd, tm):
    p, half = xg.shape
    d = 2 * half
    dff = wg.shape[2]
    tf = _divisor_tile(dff, 512, LANE)
    nf = dff // tf
    nb = p // tm

    def row_map(i, f, be, nu):
        return (jnp.minimum(i, nu[0] - 1), 0)

    def f_eff(i, f, nu):
        return jnp.where(i < nu[0], f, nf - 1)

    grid_spec = pltpu.PrefetchScalarGridSpec(
        num_scalar_prefetch=2,
        grid=(nb, nf),
        in_specs=[
            pl.BlockSpec((tm, half), row_map),
            pl.BlockSpec((1, d, tf), lambda i, f, be, nu: (be[i], 0, f_eff(i, f, nu))),
            pl.BlockSpec((1, d, tf), lambda i, f, be, nu: (be[i], 0, f_eff(i, f, nu))),
            pl.BlockSpec((1, tf, d), lambda i, f, be, nu: (be[i], f_eff(i, f, nu), 0)),
        ],
        out_specs=pl.BlockSpec((tm, d), lambda i, f, be, nu: (i, 0)),
        scratch_shapes=[pltpu.VMEM((tm, d), BF16)],
    )
    return pl.pallas_call(
        _moe_ffn_kernel,
        grid_spec=grid_spec,
        out_shape=jax.ShapeDtypeStruct((p, d), F32),
        compiler_params=_cparams(("arbitrary", "arbitrary")),
        name="moe_ffn",
    )(blk_e, nblk, xg, wg, wu, wd)


def _combine_kernel(d_ref, route_ref, h_ref, y_hbm, o_ref, y0_ref, y1_ref, sem, *, rows, lp, npad, nbatch):
    def start(r, carry):
        _row_copy(y_hbm, d_ref[0, 0, 2 * r], y0_ref, r, sem).start(priority=0)
        _row_copy(y_hbm, d_ref[0, 0, 2 * r + 1], y1_ref, r, sem).start(priority=1)
        return carry

    lax.fori_loop(0, rows, start, 0, unroll=4)
    pltpu.make_async_copy(y_hbm.at[pl.ds(0, rows)], y0_ref, sem).wait()
    pltpu.make_async_copy(y_hbm.at[pl.ds(0, rows)], y1_ref, sem).wait()

    g0 = route_ref[:, 2:3]
    g1 = route_ref[:, 3:4]
    y = g0 * y0_ref[...] + g1 * y1_ref[...]
    valid = _row_valid(pl.program_id(0) * rows, rows, lp, npad, nbatch)
    o_ref[...] = h_ref[...] + jnp.where(valid, y, 0.0)


def _moe_combine(h, route, dest, yg, rows, lp, npad, nbatch, drop_lead):
    n, d = h.shape
    nb = n // rows
    d3 = dest.reshape(nb, 1, 2 * rows)
    tpb = lp // rows
    if drop_lead:
        out_rows = n - nbatch * rows

        def out_map(i):
            return ((i // tpb) * (tpb - 1) + jnp.maximum(i % tpb - 1, 0), 0)
    else:
        out_rows = n

        def out_map(i):
            return (i, 0)

    return pl.pallas_call(
        functools.partial(_combine_kernel, rows=rows, lp=lp, npad=npad, nbatch=nbatch),
        grid=(nb,),
        in_specs=[
            pl.BlockSpec((1, 1, 2 * rows), lambda i: (i, 0, 0), memory_space=pltpu.SMEM),
            pl.BlockSpec((rows, LANE), lambda i: (i, 0)),
            pl.BlockSpec((rows, d), lambda i: (i, 0)),
            pl.BlockSpec(memory_space=pl.ANY),
        ],
        out_specs=pl.BlockSpec((rows, d), out_map),
        out_shape=jax.ShapeDtypeStruct((out_rows, d), F32),
        scratch_shapes=[pltpu.VMEM((rows, d), F32), pltpu.VMEM((rows, d), F32), pltpu.SemaphoreType.DMA(())],
        compiler_params=_cparams(("arbitrary",)),
        name="moe_combine",
    )(d3, route, h, yg)


def _moe_layout(experts, tm, nblocks):
    n = experts.shape[0]
    flat_e = experts.reshape(-1)
    onehot = (flat_e[:, None] == jnp.arange(N_EXPERTS, dtype=jnp.int32)[None, :]).astype(jnp.int32)
    csum = jnp.cumsum(onehot, axis=0)
    pos = jnp.take_along_axis(csum, flat_e[:, None], axis=1)[:, 0] - 1
    counts = csum[-1]
    padded = (counts + tm - 1) // tm * tm
    ends = jnp.cumsum(padded)
    gstart = ends - padded
    dest = (gstart[flat_e] + pos).astype(jnp.int32)
    flat_tok = jnp.arange(2 * n, dtype=jnp.int32) // 2
    slot_tok = jnp.zeros((nblocks * tm,), jnp.int32).at[dest].set(flat_tok)
    nblk = (ends[-1] // tm).astype(jnp.int32)
    blk = jnp.minimum(jnp.arange(nblocks, dtype=jnp.int32), nblk - 1)
    blk_e = jnp.minimum(jnp.searchsorted(ends, blk * tm, side="right"), N_EXPERTS - 1).astype(jnp.int32)
    return dest, slot_tok, blk_e, nblk.reshape(1)


def _moe_block(h, gain, rw, wg, wu, wd, lp, npad, nbatch, lead_rows):
    n, d = h.shape
    xp, route = _router(h, gain, rw)
    experts = route[:, :2].astype(jnp.int32)
    tm = _divisor_tile(n, 1024, GATHER_ROWS) if n % GATHER_ROWS == 0 else n
    rows = min(GATHER_ROWS, tm)
    nblocks = -(-(2 * n + N_EXPERTS * (tm - 1)) // tm)
    dest, slot_tok, blk_e, nblk = _moe_layout(experts, tm, nblocks)
    xg = _moe_gather(xp, slot_tok, rows)
    yg = _moe_ffn(xg, blk_e, nblk, wg, wu, wd, tm)
    return _moe_combine(h, route, dest, yg, rows, lp, npad, nbatch, drop_lead=(lead_rows == rows))


def _inproj_layout(d_model):
    att_w = HEADS * ATT_DH
    qk_w = HEADS * ML_DK
    v_w = HEADS * ML_DV
    names = ("qa", "ka", "va", "fa", "qm", "km", "vm", "im", "fm", "om", "ga", "gm")
    widths = (att_w, att_w, att_w, HEADS, qk_w, qk_w, v_w, HEADS, HEADS, v_w, d_model, d_model)
    src, off = {}, 0
    for nm, wd in zip(names, widths):
        src[nm] = (off, wd)
        off += wd
    big_order = ("vm", "om", "ga", "gm", "qa", "ka", "va", "qm", "km")
    cols, off = {}, 0
    for nm in big_order:
        cols[nm] = off
        off += src[nm][1]
    return src, big_order, cols


def kernel(x, meta_tokens, ln_mix, w_in, att_q_norm, att_k_norm, att_f_bias, ml_conv_w, ml_conv_b, ml_i_bias,
           ml_f_bias, ml_h_norm, w_branch_att, w_branch_ml, w_out, ln_ffn, dense_w_gate, dense_w_up, dense_w_down,
           router_w, moe_w_gate, moe_w_up, moe_w_down):
    nbatch, seq, d = x.shape
    depth = w_in.shape[0]
    lp = -(-(CHUNK + seq) // SEQ_TILE) * SEQ_TILE
    npad = lp - seq - N_META
    n = nbatch * lp
    src, big_order, cols = _inproj_layout(d)

    h = jnp.concatenate([
        jnp.zeros((nbatch, npad, d), x.dtype),
        jnp.broadcast_to(meta_tokens[None].astype(x.dtype), (nbatch, N_META, d)),
        x], axis=1).reshape(n, d)

    for layer in range(depth):
        wl = w_in[layer]
        w_big = jnp.concatenate([wl[:, src[nm][0]:src[nm][0] + src[nm][1]] for nm in big_order], axis=1).astype(BF16)
        w_gates = jnp.concatenate(
            [wl[:, src[nm][0]:src[nm][0] + HEADS] for nm in ("fa", "fm", "im")]
            + [jnp.zeros((d, LANE - 3 * HEADS), F32)], axis=1)
        gate_bias = jnp.concatenate([att_f_bias[layer], ml_f_bias[layer], ml_i_bias[layer],
                                     jnp.zeros((LANE - 3 * HEADS,), F32)]).reshape(1, LANE)

        proj, gates = _norm_inproj(h, ln_mix[layer].reshape(1, d), w_big, w_gates)
        proj3 = proj.reshape(nbatch, lp, -1)
        gcol, grow = _gate_prep(gates.reshape(nbatch, lp, LANE), gate_bias, npad)
        qa, ka, qm, kmt = _qk_prep(proj3, cols, att_q_norm[layer].reshape(1, ATT_DH),
                                   att_k_norm[layer].reshape(1, ATT_DH), ml_conv_w[layer],
                                   ml_conv_b[layer].reshape(1, -1), npad)
        c_rows = grow[:, :HEADS, :].reshape(nbatch * HEADS, lp // SEQ_TILE, 1, SEQ_TILE)
        y_att = _fox_attention(qa, ka, proj3, cols["va"], c_rows, npad)
        y_ml = _mlstm(qm, kmt, proj3, cols, gcol, grow, ml_h_norm[layer].reshape(1, -1))
        merged = _merge(y_att.reshape(n, -1), y_ml.reshape(n, -1), w_branch_att[layer].astype(BF16),
                        w_branch_ml[layer].astype(BF16), proj, cols)
        h = _out_residual(merged, w_out[layer].astype(BF16), h, lp, npad, nbatch)

        j = layer // 2
        gain = ln_ffn[layer].reshape(1, d)
        if layer % 2 == 0:
            h = _ffn(h, gain, dense_w_gate[j].astype(BF16), dense_w_up[j].astype(BF16),
                     dense_w_down[j].astype(BF16), lp, npad, nbatch)
        else:
            rw = jnp.concatenate([router_w[j], jnp.zeros((d, LANE - N_EXPERTS), F32)], axis=1)
            h = _moe_block(h, gain, rw, moe_w_gate[j].astype(BF16), moe_w_up[j].astype(BF16),
                           moe_w_down[j].astype(BF16), lp, npad, nbatch,
                           lead_rows=(lp - seq) if layer == depth - 1 else 0)

    if h.shape[0] == nbatch * seq:
        return h.reshape(nbatch, seq, d)
    return h.reshape(nbatch, lp, d)[:, lp - seq:, :]
```

```python
import functools

import jax
import jax.numpy as jnp
from jax import lax
from jax.experimental import pallas as pl
from jax.experimental.pallas import tpu as pltpu

N_META = 16
CHUNK = 128
HEADS = 8
ATT_DH = 128
ML_DK = 128
ML_DV = 256
CONV_WIDTH = 4
N_EXPERTS = 8
EPS = 1e-6
NEG = -1e30
LOG2E = 1.4426950408889634

LANE = 128
SUBLANE = 8
VMEM_BYTES_V7X = 64 * 1024 * 1024
VMEM_LIMIT = 56 * 1024 * 1024

SEQ_TILE = 512
HALO = SUBLANE

F32 = jnp.float32
BF16 = jnp.bfloat16
HIGHEST = lax.Precision.HIGHEST


def _divisor_tile(n, pref, align):
    if n <= pref:
        return n
    t = (pref // align) * align
    while t >= align:
        if n % t == 0:
            return t
        t -= align
    return n


def _cparams(sem):
    return pltpu.CompilerParams(dimension_semantics=sem, vmem_limit_bytes=VMEM_LIMIT)


def _row_valid(row0, nrows, lp, npad, nbatch):
    r = row0 + lax.broadcasted_iota(jnp.int32, (nrows, 1), 0)
    invalid = (r >= 0) & (r < npad)
    for b in range(1, nbatch):
        invalid = invalid | ((r >= b * lp) & (r < b * lp + npad))
    return jnp.logical_not(invalid)


def _log_sigmoid(x):
    return jnp.minimum(x, 0.0) - jnp.log1p(jnp.exp(-jnp.abs(x)))


def _sigmoid(x):
    return 1.0 / (1.0 + jnp.exp(-x))


def _dot_f32_3pass(x, w):
    xh = x.astype(BF16)
    xl = (x - xh.astype(F32)).astype(BF16)
    wh = w.astype(BF16)
    wl = (w - wh.astype(F32)).astype(BF16)
    return (jnp.dot(xh, wh, preferred_element_type=F32) + jnp.dot(xl, wh, preferred_element_type=F32)
            + jnp.dot(xh, wl, preferred_element_type=F32))


def _norm_inproj_kernel(x_ref, g_ref, w_ref, wg_ref, o_ref, og_ref, xn_ref):
    @pl.when(pl.program_id(1) == 0)
    def _():
        x = x_ref[...]
        ms = jnp.mean(x * x, axis=-1, keepdims=True)
        xn = x * lax.rsqrt(ms + EPS) * g_ref[...]
        xn_ref[...] = xn.astype(BF16)
        og_ref[...] = _dot_f32_3pass(xn, wg_ref[...])

    o_ref[...] = jnp.dot(xn_ref[...], w_ref[...], preferred_element_type=F32).astype(o_ref.dtype)


def _norm_inproj(h, gain, w_big, w_gates):
    n, d = h.shape
    wn = w_big.shape[1]
    tm = _divisor_tile(n, 1024, SUBLANE)
    tn = _divisor_tile(wn, 1024, LANE)
    return pl.pallas_call(
        _norm_inproj_kernel,
        grid=(n // tm, wn // tn),
        in_specs=[
            pl.BlockSpec((tm, d), lambda i, j: (i, 0)),
            pl.BlockSpec((1, d), lambda i, j: (0, 0)),
            pl.BlockSpec((d, tn), lambda i, j: (0, j)),
            pl.BlockSpec((d, LANE), lambda i, j: (0, 0)),
        ],
        out_specs=[
            pl.BlockSpec((tm, tn), lambda i, j: (i, j)),
            pl.BlockSpec((tm, LANE), lambda i, j: (i, 0)),
        ],
        out_shape=[jax.ShapeDtypeStruct((n, wn), BF16), jax.ShapeDtypeStruct((n, LANE), F32)],
        scratch_shapes=[pltpu.VMEM((tm, d), BF16)],
        compiler_params=_cparams(("parallel", "arbitrary")),
        name="norm_inproj",
    )(h, gain, w_big, w_gates)


def _gate_prep_kernel(g_ref, b_ref, col_ref, row_ref, carry_ref, *, npad):
    c = pl.program_id(1)

    @pl.when(c == 0)
    def _():
        carry_ref[...] = jnp.zeros_like(carry_ref)

    lane = lax.broadcasted_iota(jnp.int32, (CHUNK, LANE), 1)
    r_i = lax.broadcasted_iota(jnp.int32, (CHUNK, CHUNK), 0)
    c_i = lax.broadcasted_iota(jnp.int32, (CHUNK, CHUNK), 1)
    tril = (r_i >= c_i).astype(F32)
    carry = carry_ref[0:1, :]
    for u in range(g_ref.shape[1] // CHUNK):
        rows = slice(u * CHUNK, (u + 1) * CHUNK)
        x = g_ref[0, rows, :] + b_ref[...]
        pos = c * g_ref.shape[1] + u * CHUNK + lax.broadcasted_iota(jnp.int32, (CHUNK, LANE), 0)
        valid = pos >= npad
        ls = _log_sigmoid(x)
        z = jnp.where(lane < HEADS, ls, jnp.where((lane < 2 * HEADS) & valid, ls, 0.0))
        s = jnp.dot(tril, z, preferred_element_type=F32, precision=HIGHEST)
        s = s + jnp.where(lane < HEADS, carry, 0.0)
        carry = s[CHUNK - 1:CHUNK, :]
        ipre = jnp.where(valid, x, -jnp.inf)
        col = jnp.where(lane < HEADS, s * LOG2E,
                        jnp.where(lane < 2 * HEADS, s, jnp.where(lane < 3 * HEADS, ipre, 0.0)))
        col_ref[0, rows, :] = col
        row_ref[0, :, rows] = col.T[:3 * HEADS, :]
    carry_ref[...] = jnp.broadcast_to(carry, carry_ref.shape)


def _gate_prep(gates3, bias, npad):
    b, lp, _ = gates3.shape
    rows = SEQ_TILE
    nc = lp // rows
    return pl.pallas_call(
        functools.partial(_gate_prep_kernel, npad=npad),
        grid=(b, nc),
        in_specs=[
            pl.BlockSpec((1, rows, LANE), lambda bi, c: (bi, c, 0)),
            pl.BlockSpec((1, LANE), lambda bi, c: (0, 0)),
        ],
        out_specs=[
            pl.BlockSpec((1, rows, LANE), lambda bi, c: (bi, c, 0)),
            pl.BlockSpec((1, 3 * HEADS, rows), lambda bi, c: (bi, 0, c)),
        ],
        out_shape=[jax.ShapeDtypeStruct((b, lp, LANE), F32), jax.ShapeDtypeStruct((b, 3 * HEADS, lp), F32)],
        scratch_shapes=[pltpu.VMEM((SUBLANE, LANE), F32)],
        compiler_params=_cparams(("parallel", "arbitrary")),
        name="gate_prep",
    )(gates3, bias)


def _qk_prep_kernel(qa_ref, ka_ref, qm_ref, km_ref, qmh_ref, kmh_ref, gq_ref, gk_ref, cw_ref, cb_ref,
                    qa_o, ka_o, qm_o, kmt_o, win_ref, *, npad):
    c = pl.program_id(1)
    ml_w = HEADS * ML_DK

    for hh in range(HEADS):
        sl = slice(hh * ATT_DH, (hh + 1) * ATT_DH)
        q = qa_ref[0, :, sl].astype(F32)
        q = q * lax.rsqrt(jnp.mean(q * q, axis=-1, keepdims=True) + EPS) * gq_ref[...]
        qa_o[0, :, sl] = (q * (ATT_DH ** -0.5 * LOG2E)).astype(BF16)
        k = ka_ref[0, :, sl].astype(F32)
        k = k * lax.rsqrt(jnp.mean(k * k, axis=-1, keepdims=True) + EPS) * gk_ref[...]
        ka_o[0, :, sl] = k.astype(BF16)

    pos = c * CHUNK + lax.broadcasted_iota(jnp.int32, (CHUNK, 1), 0)
    hpos = c * CHUNK - HALO + lax.broadcasted_iota(jnp.int32, (HALO, 1), 0)
    row_ok = pos >= npad
    halo_ok = (hpos >= npad) & (c > 0)

    r_i = lax.broadcasted_iota(jnp.int32, (CHUNK, CHUNK), 0)
    c_i = lax.broadcasted_iota(jnp.int32, (CHUNK, CHUNK), 1)
    shift = [(r_i - c_i == CONV_WIDTH - 1 - j).astype(BF16) for j in range(CONV_WIDTH - 1)]

    def conv_silu(x_ref, xh_ref, col0):
        def tap(j):
            return cw_ref[j:j + 1, col0:col0 + ml_w]

        bias = cb_ref[:, col0:col0 + ml_w]
        xf = jnp.where(row_ok, x_ref[0].astype(F32), 0.0)
        xm = xf.astype(BF16)
        y = bias + tap(CONV_WIDTH - 1) * xf
        for j in range(CONV_WIDTH - 1):
            y = y + tap(j) * jnp.dot(shift[j], xm, preferred_element_type=F32)
        win_ref[0:HALO, :] = jnp.where(halo_ok, xh_ref[0].astype(F32), 0.0)
        win_ref[HALO:2 * HALO, :] = xf[0:HALO, :]
        y_head = jnp.broadcast_to(bias, (HALO, ml_w))
        for j in range(CONV_WIDTH):
            off = HALO - (CONV_WIDTH - 1) + j
            y_head = y_head + tap(j) * win_ref[off:off + HALO, :]
        y = jnp.concatenate([y_head, y[HALO:, :]], axis=0)
        return y * _sigmoid(y)

    yq = conv_silu(qm_ref, qmh_ref, 0)
    qm_o[0] = yq.astype(BF16)
    yk = conv_silu(km_ref, kmh_ref, ml_w) * (ML_DK ** -0.5)
    for hh in range(HEADS):
        sl = slice(hh * ML_DK, (hh + 1) * ML_DK)
        kmt_o[0, sl, :] = yk[:, sl].T.astype(BF16)


def _qk_prep(proj3, cols, gq, gk, conv_w, conv_b, npad):
    b, lp, _ = proj3.shape
    nc = lp // CHUNK
    w = HEADS * ATT_DH
    qa_c, ka_c, qm_c, km_c = (cols[k] // w for k in ("qa", "ka", "qm", "km"))
    hpc = CHUNK // HALO

    def main(cb):
        return pl.BlockSpec((1, CHUNK, w), lambda bi, c: (bi, c, cb))

    def halo(cb):
        return pl.BlockSpec((1, HALO, w), lambda bi, c: (bi, jnp.maximum(c * hpc - 1, 0), cb))

    full = lambda shape: pl.BlockSpec(shape, lambda bi, c: (0,) * len(shape))
    return pl.pallas_call(
        functools.partial(_qk_prep_kernel, npad=npad),
        grid=(b, nc),
        in_specs=[main(qa_c), main(ka_c), main(qm_c), main(km_c), halo(qm_c), halo(km_c),
                  full((1, ATT_DH)), full((1, ATT_DH)), full((CONV_WIDTH, 2 * w)), full((1, 2 * w))],
        out_specs=[
            pl.BlockSpec((1, CHUNK, w), lambda bi, c: (bi, c, 0)),
            pl.BlockSpec((1, CHUNK, w), lambda bi, c: (bi, c, 0)),
            pl.BlockSpec((1, CHUNK, w), lambda bi, c: (bi, c, 0)),
            pl.BlockSpec((1, w, CHUNK), lambda bi, c: (bi, 0, c)),
        ],
        out_shape=[jax.ShapeDtypeStruct((b, lp, w), BF16)] * 3 + [jax.ShapeDtypeStruct((b, w, lp), BF16)],
        scratch_shapes=[pltpu.VMEM((2 * HALO, w), F32)],
        compiler_params=_cparams(("parallel", "arbitrary")),
        name="qk_prep",
    )(proj3, proj3, proj3, proj3, proj3, proj3, gq, gk, conv_w, conv_b)


FOX_PART_ROWS = 512
FOX_Q_TILE_MAX = 1536


def _fox_kernel(q_ref, k_ref, v_ref, c_ref, o_ref, m_ref, acc_ref, s0_ref, s1_ref, *, tq, tk, npad):
    i = pl.program_id(2)
    r = tq // tk
    hq = min(FOX_PART_ROWS, tk)
    slots = (s0_ref, s1_ref)
    c0 = c_ref[0, r * i][:, 0:1]
    m_ref[...] = jnp.full_like(m_ref, NEG)
    acc_ref[...] = jnp.zeros_like(acc_ref)
    ones = jnp.ones((tk, LANE), BF16)

    def kv_start(j):
        return j * tk if isinstance(j, int) else pl.multiple_of(j * tk, tk)

    def scores(j, s_ref, masked, row0):
        k = k_ref[0, pl.ds(kv_start(j), tk), :]
        s = lax.dot_general(q_ref[0, row0:, :], k, (((1,), (1,)), ((), ())), preferred_element_type=F32)
        s = s + (c0 - c_ref[0, j])
        if masked:
            qpos = i * tq + row0 + lax.broadcasted_iota(jnp.int32, (tq - row0, tk), 0)
            kpos = j * tk + lax.broadcasted_iota(jnp.int32, (tq - row0, tk), 1)
            s = jnp.where((kpos <= qpos) & (kpos >= npad), s, NEG)
        s_ref[row0:, :] = s

    def accumulate(j, s_ref, row0):
        vext = jnp.concatenate([v_ref[0, pl.ds(kv_start(j), tk), :], ones], axis=1)
        for part in range(row0 // hq, tq // hq):
            rows = slice(part * hq, (part + 1) * hq)
            s = s_ref[rows, :]
            m_prev = m_ref[rows, :]
            m_new = jnp.maximum(m_prev, jnp.broadcast_to(jnp.max(s, axis=1, keepdims=True), (hq, LANE)))
            alpha = jnp.exp2(m_prev - m_new)
            p = jnp.exp2(s - jnp.concatenate([m_new] * (tk // LANE), axis=1))
            pv = jnp.dot(p.astype(BF16), vext, preferred_element_type=F32)
            acc_ref[rows, :] = jnp.concatenate([alpha, alpha], axis=1) * acc_ref[rows, :] + pv
            m_ref[rows, :] = m_new

    def run(chunks):
        for n, (j, d) in enumerate(chunks):
            if n + 1 < len(chunks):
                jn, dn = chunks[n + 1]
                scores(jn, slots[(n + 1) % 2], dn is not None, 0 if dn is None else dn * tk)
            accumulate(j, slots[n % 2], 0 if d is None else d * tk)

    first = r * i
    diag = [(first + d, d) for d in range(r)]
    scores(0, s0_ref, True, 0)

    n_plain = jnp.maximum(first - 1, 0)
    odd = lax.bitwise_and(n_plain, 1)

    def pair_body(p, carry):
        j = 2 * p
        scores(j + 1, s1_ref, False, 0)
        accumulate(j, s0_ref, 0)
        scores(j + 2, s0_ref, False, 0)
        accumulate(j + 1, s1_ref, 0)
        return carry

    lax.fori_loop(0, lax.shift_right_logical(n_plain, 1), pair_body, 0)

    @pl.when(i == 0)
    def _():
        run(diag)

    @pl.when((i > 0) & (odd == 0))
    def _():
        run([(first - 1, None)] + diag)

    @pl.when((i > 0) & (odd == 1))
    def _():
        run([(first - 2, None), (first - 1, None)] + diag)

    o_ref[0] = (acc_ref[:, :ATT_DH] / acc_ref[:, ATT_DH:]).astype(o_ref.dtype)


def _fox_attention(qa, ka, proj3, v_col, c_rows, npad):
    b, lp, w = qa.shape
    tk = SEQ_TILE
    tq = _divisor_tile(lp, FOX_Q_TILE_MAX, tk)
    nq = lp // tq
    nk = lp // tk
    vcb = v_col // ATT_DH
    return pl.pallas_call(
        functools.partial(_fox_kernel, tq=tq, tk=tk, npad=npad),
        grid=(b, HEADS, nq),
        in_specs=[
            pl.BlockSpec((1, tq, ATT_DH), lambda bi, h, i: (bi, i, h)),
            pl.BlockSpec((1, lp, ATT_DH), lambda bi, h, i: (bi, 0, h)),
            pl.BlockSpec((1, lp, ATT_DH), lambda bi, h, i: (bi, 0, vcb + h)),
            pl.BlockSpec((1, nk, 1, tk), lambda bi, h, i: (bi * HEADS + h, 0, 0, 0)),
        ],
        out_specs=pl.BlockSpec((1, tq, ATT_DH), lambda bi, h, i: (bi, i, h)),
        out_shape=jax.ShapeDtypeStruct((b, lp, w), BF16),
        scratch_shapes=[pltpu.VMEM((tq, LANE), F32), pltpu.VMEM((tq, ATT_DH + LANE), F32),
                        pltpu.VMEM((tq, tk), F32), pltpu.VMEM((tq, tk), F32)],
        compiler_params=_cparams(("parallel", "parallel", "arbitrary")),
        name="fox_attention",
    )(qa, ka, proj3, c_rows)


def _mlstm_kernel(q_ref, kt_ref, v_ref, om_ref, col_ref, row_ref, hn_ref, o_ref, ct_ref, m_ref):
    c = pl.program_id(1)

    @pl.when(c == 0)
    def _():
        ct_ref[...] = jnp.zeros_like(ct_ref)
        m_ref[...] = jnp.zeros_like(m_ref)

    t = CHUNK
    r_i = lax.broadcasted_iota(jnp.int32, (t, t), 0)
    c_i = lax.broadcasted_iota(jnp.int32, (t, t), 1)
    causal = r_i >= c_i
    ones = jnp.ones((t, LANE), BF16)

    for hh in range(HEADS):
        q = q_ref[0, :, hh * ML_DK:(hh + 1) * ML_DK]
        kt = kt_ref[0, hh * ML_DK:(hh + 1) * ML_DK, :]
        v = v_ref[0, :, hh * ML_DV:(hh + 1) * ML_DV]
        vext = jnp.concatenate([v, ones], axis=1)
        bt_c = col_ref[0, :, HEADS + hh:HEADS + hh + 1]
        bt_r = row_ref[0, HEADS + hh:HEADS + hh + 1, :]
        it_r = row_ref[0, 2 * HEADS + hh:2 * HEADS + hh + 1, :]
        g = bt_r[:, t - 1:t]
        m = m_ref[hh, 0:1, 0:1]

        dmat = jnp.where(causal, bt_c - bt_r + it_r, -jnp.inf)
        inter = bt_c + m
        m_row = jnp.maximum(inter, jnp.max(dmat, axis=1, keepdims=True))
        w_intra = jnp.exp(dmat - m_row)
        w_inter = jnp.exp(inter - m_row)
        qk = jnp.dot(q, kt, preferred_element_type=F32)
        a = (qk * w_intra).astype(BF16)
        ct = ct_ref[hh]
        nd = (jnp.dot(a, vext, preferred_element_type=F32)
              + w_inter * jnp.dot(q, ct.astype(BF16), preferred_element_type=F32))
        num = nd[:, :ML_DV]
        den = nd[:, ML_DV:ML_DV + 1]
        denom = jnp.maximum(jnp.abs(den), jnp.exp(-m_row))
        hv = num / denom

        src = g - bt_r + it_r
        m_new = jnp.maximum(g + m, jnp.max(src, axis=1, keepdims=True))
        w_src = jnp.exp(src - m_new)
        decay = jnp.exp(g + m - m_new)
        ks = (kt.astype(F32) * w_src).astype(BF16)
        ct_ref[hh] = decay * ct + jnp.dot(ks, vext, preferred_element_type=F32)
        m_ref[hh] = jnp.broadcast_to(m_new, (SUBLANE, LANE))

        hv = hv * lax.rsqrt(jnp.mean(hv * hv, axis=-1, keepdims=True) + EPS)
        hv = hv * hn_ref[:, hh * ML_DV:(hh + 1) * ML_DV]
        og = _sigmoid(om_ref[0, :, hh * ML_DV:(hh + 1) * ML_DV].astype(F32))
        o_ref[0, :, hh * ML_DV:(hh + 1) * ML_DV] = (og * hv).astype(o_ref.dtype)


def _mlstm(qm, kmt, proj3, cols, gcol, grow, hnorm):
    b, lp, qw = qm.shape
    nc = lp // CHUNK
    vw = HEADS * ML_DV
    v_cb = cols["vm"] // vw
    om_cb = cols["om"] // vw
    return pl.pallas_call(
        _mlstm_kernel,
        grid=(b, nc),
        in_specs=[
            pl.BlockSpec((1, CHUNK, qw), lambda bi, c: (bi, c, 0)),
            pl.BlockSpec((1, qw, CHUNK), lambda bi, c: (bi, 0, c)),
            pl.BlockSpec((1, CHUNK, vw), lambda bi, c: (bi, c, v_cb)),
            pl.BlockSpec((1, CHUNK, vw), lambda bi, c: (bi, c, om_cb)),
            pl.BlockSpec((1, CHUNK, LANE), lambda bi, c: (bi, c, 0)),
            pl.BlockSpec((1, 3 * HEADS, CHUNK), lambda bi, c: (bi, 0, c)),
            pl.BlockSpec((1, vw), lambda bi, c: (0, 0)),
        ],
        out_specs=pl.BlockSpec((1, CHUNK, vw), lambda bi, c: (bi, c, 0)),
        out_shape=jax.ShapeDtypeStruct((b, lp, vw), BF16),
        scratch_shapes=[pltpu.VMEM((HEADS, ML_DK, ML_DV + LANE), F32), pltpu.VMEM((HEADS, SUBLANE, LANE), F32)],
        compiler_params=_cparams(("parallel", "arbitrary")),
        name="mlstm",
    )(qm, kmt, proj3, proj3, gcol, grow, hnorm)


def _merge_kernel(ya_ref, ym_ref, wa_ref, wm_ref, ga_ref, gm_ref, o_ref):
    a = jnp.dot(ya_ref[...], wa_ref[...], preferred_element_type=F32)
    m = jnp.dot(ym_ref[...], wm_ref[...], preferred_element_type=F32)
    out = _sigmoid(ga_ref[...].astype(F32)) * a + _sigmoid(gm_ref[...].astype(F32)) * m
    o_ref[...] = out.astype(o_ref.dtype)


def _merge(y_att, y_ml, wa, wm, proj, cols):
    n, ka = y_att.shape
    km = y_ml.shape[1]
    d = wa.shape[1]
    tm = _divisor_tile(n, 1024, SUBLANE)
    tn = _divisor_tile(d, 1024, LANE)
    ga_cb = cols["ga"] // tn
    gm_cb = cols["gm"] // tn
    return pl.pallas_call(
        _merge_kernel,
        grid=(n // tm, d // tn),
        in_specs=[
            pl.BlockSpec((tm, ka), lambda i, j: (i, 0)),
            pl.BlockSpec((tm, km), lambda i, j: (i, 0)),
            pl.BlockSpec((ka, tn), lambda i, j: (0, j)),
            pl.BlockSpec((km, tn), lambda i, j: (0, j)),
            pl.BlockSpec((tm, tn), lambda i, j: (i, ga_cb + j)),
            pl.BlockSpec((tm, tn), lambda i, j: (i, gm_cb + j)),
        ],
        out_specs=pl.BlockSpec((tm, tn), lambda i, j: (i, j)),
        out_shape=jax.ShapeDtypeStruct((n, d), BF16),
        compiler_params=_cparams(("parallel", "arbitrary")),
        name="merge",
    )(y_att, y_ml, wa, wm, proj, proj)


def _out_residual_kernel(x_ref, w_ref, h_ref, o_ref, *, tm, lp, npad, nbatch):
    mix = jnp.dot(x_ref[...], w_ref[...], preferred_element_type=F32)
    valid = _row_valid(pl.program_id(0) * tm, tm, lp, npad, nbatch)
    o_ref[...] = h_ref[...] + jnp.where(valid, mix, 0.0)


def _out_residual(x, w, h, lp, npad, nbatch):
    n, k = x.shape
    d = w.shape[1]
    tm = _divisor_tile(n, 1024, SUBLANE)
    tn = _divisor_tile(d, 1024, LANE)
    return pl.pallas_call(
        functools.partial(_out_residual_kernel, tm=tm, lp=lp, npad=npad, nbatch=nbatch),
        grid=(n // tm, d // tn),
        in_specs=[
            pl.BlockSpec((tm, k), lambda i, j: (i, 0)),
            pl.BlockSpec((k, tn), lambda i, j: (0, j)),
            pl.BlockSpec((tm, tn), lambda i, j: (i, j)),
        ],
        out_specs=pl.BlockSpec((tm, tn), lambda i, j: (i, j)),
        out_shape=jax.ShapeDtypeStruct((n, d), F32),
        compiler_params=_cparams(("parallel", "arbitrary")),
        name="out_residual",
    )(x, w, h)


def _ffn_kernel(x_ref, g_ref, wg_ref, wu_ref, wd_ref, o_ref, xn_ref, *, tm, lp, npad, nbatch):
    f = pl.program_id(1)

    @pl.when(f == 0)
    def _():
        x = x_ref[...]
        ms = jnp.mean(x * x, axis=-1, keepdims=True)
        xn_ref[...] = (x * lax.rsqrt(ms + EPS) * g_ref[...]).astype(BF16)
        o_ref[...] = jnp.zeros_like(o_ref)

    xn = xn_ref[...]
    gate = jnp.dot(xn, wg_ref[...], preferred_element_type=F32)
    up = jnp.dot(xn, wu_ref[...], preferred_element_type=F32)
    act = (gate * _sigmoid(gate) * up).astype(BF16)
    o_ref[...] += jnp.dot(act, wd_ref[...], preferred_element_type=F32)

    @pl.when(f == pl.num_programs(1) - 1)
    def _():
        valid = _row_valid(pl.program_id(0) * tm, tm, lp, npad, nbatch)
        o_ref[...] = x_ref[...] + jnp.where(valid, o_ref[...], 0.0)


def _ffn(h, gain, wg, wu, wd, lp, npad, nbatch):
    n, d = h.shape
    dff = wg.shape[1]
    tm = _divisor_tile(n, 1024, SUBLANE)
    tf = _divisor_tile(dff, 512, LANE)
    return pl.pallas_call(
        functools.partial(_ffn_kernel, tm=tm, lp=lp, npad=npad, nbatch=nbatch),
        grid=(n // tm, dff // tf),
        in_specs=[
            pl.BlockSpec((tm, d), lambda i, f: (i, 0)),
            pl.BlockSpec((1, d), lambda i, f: (0, 0)),
            pl.BlockSpec((d, tf), lambda i, f: (0, f)),
            pl.BlockSpec((d, tf), lambda i, f: (0, f)),
            pl.BlockSpec((tf, d), lambda i, f: (f, 0)),
        ],
        out_specs=pl.BlockSpec((tm, d), lambda i, f: (i, 0)),
        out_shape=jax.ShapeDtypeStruct((n, d), F32),
        scratch_shapes=[pltpu.VMEM((tm, d), BF16)],
        compiler_params=_cparams(("parallel", "arbitrary")),
        name="ffn",
    )(h, gain, wg, wu, wd)


def _bf16_bits(x):
    bits = lax.bitcast_convert_type(x, jnp.uint32)
    rounded = bits + jnp.uint32(0x7FFF) + ((bits >> 16) & jnp.uint32(1))
    return rounded >> 16


def _router_kernel(x_ref, g_ref, rw_ref, xp_ref, route_ref):
    x = x_ref[...]
    d = x.shape[1]
    ms = jnp.mean(x * x, axis=-1, keepdims=True)
    xn = x * lax.rsqrt(ms + EPS) * g_ref[...]
    xp_ref[...] = _bf16_bits(xn[:, :d // 2]) | (_bf16_bits(xn[:, d // 2:]) << 16)

    logits = _dot_f32_3pass(xn, rw_ref[...])
    lane = lax.broadcasted_iota(jnp.int32, logits.shape, 1)
    logits = jnp.where(lane < N_EXPERTS, logits, -jnp.inf)
    v1 = jnp.max(logits, axis=-1, keepdims=True)
    i1 = jnp.min(jnp.where(logits == v1, lane, LANE), axis=-1, keepdims=True)
    rest = jnp.where(lane == i1, -jnp.inf, logits)
    v2 = jnp.max(rest, axis=-1, keepdims=True)
    i2 = jnp.min(jnp.where(rest == v2, lane, LANE), axis=-1, keepdims=True)
    e = jnp.exp(v2 - v1)
    g1 = 1.0 / (1.0 + e)
    g2 = e / (1.0 + e)
    route = jnp.where(lane == 0, i1.astype(F32),
                      jnp.where(lane == 1, i2.astype(F32),
                                jnp.where(lane == 2, g1, jnp.where(lane == 3, g2, 0.0))))
    route_ref[...] = route


def _router(h, gain, rw):
    n, d = h.shape
    tm = _divisor_tile(n, 1024, SUBLANE)
    return pl.pallas_call(
        _router_kernel,
        grid=(n // tm,),
        in_specs=[
            pl.BlockSpec((tm, d), lambda i: (i, 0)),
            pl.BlockSpec((1, d), lambda i: (0, 0)),
            pl.BlockSpec((d, LANE), lambda i: (0, 0)),
        ],
        out_specs=[
            pl.BlockSpec((tm, d // 2), lambda i: (i, 0)),
            pl.BlockSpec((tm, LANE), lambda i: (i, 0)),
        ],
        out_shape=[jax.ShapeDtypeStruct((n, d // 2), jnp.uint32), jax.ShapeDtypeStruct((n, LANE), F32)],
        compiler_params=_cparams(("parallel",)),
        name="router",
    )(h, gain, rw)


GATHER_ROWS = 512


def _row_copy(src_hbm, src_row, dst_ref, dst_row, sem):
    return pltpu.make_async_copy(src_hbm.at[pl.ds(src_row, 1)], dst_ref.at[pl.ds(dst_row, 1)], sem)


def _gather_kernel(idx_ref, x_hbm, o_ref, sem, *, rows):
    def start(r2, carry):
        for u in range(2):
            r = 2 * r2 + u
            _row_copy(x_hbm, idx_ref[0, 0, r], o_ref, r, sem).start(priority=u)
        return carry

    lax.fori_loop(0, rows // 2, start, 0, unroll=4)
    pltpu.make_async_copy(x_hbm.at[pl.ds(0, rows)], o_ref, sem).wait()


def _moe_gather(xp, slot_tok, rows):
    p = slot_tok.shape[0]
    w = xp.shape[1]
    nb = p // rows
    idx3 = slot_tok.reshape(nb, 1, rows)
    return pl.pallas_call(
        functools.partial(_gather_kernel, rows=rows),
        grid=(nb,),
        in_specs=[
            pl.BlockSpec((1, 1, rows), lambda i: (i, 0, 0), memory_space=pltpu.SMEM),
            pl.BlockSpec(memory_space=pl.ANY),
        ],
        out_specs=pl.BlockSpec((rows, w), lambda i: (i, 0)),
        out_shape=jax.ShapeDtypeStruct((p, w), xp.dtype),
        scratch_shapes=[pltpu.SemaphoreType.DMA(())],
        compiler_params=_cparams(("arbitrary",)),
        name="moe_gather",
    )(idx3, xp)


def _moe_ffn_kernel(blk_e_ref, nblk_ref, xg_ref, wg_ref, wu_ref, wd_ref, o_ref, xn_ref):
    i = pl.program_id(0)
    f = pl.program_id(1)

    @pl.when(i < nblk_ref[0])
    def _():
        @pl.when(f == 0)
        def _():
            packed = xg_ref[...]
            half = packed.shape[1]
            lo = lax.bitcast_convert_type(packed << 16, F32)
            hi = lax.bitcast_convert_type(packed & jnp.uint32(0xFFFF0000), F32)
            xn_ref[:, :half] = lo.astype(BF16)
            xn_ref[:, half:] = hi.astype(BF16)
            o_ref[...] = jnp.zeros_like(o_ref)

        xn = xn_ref[...]
        gate = jnp.dot(xn, wg_ref[0], preferred_element_type=F32)
        up = jnp.dot(xn, wu_ref[0], preferred_element_type=F32)
        act = (gate * _sigmoid(gate) * up).astype(BF16)
        o_ref[...] += jnp.dot(act, wd_ref[0], preferred_element_type=F32)

    @pl.when((i >= nblk_ref[0]) & (f == 0))
    def _():
        o_ref[...] = jnp.zeros_like(o_ref)


def _moe_ffn(xg, blk_e, nblk, wg, wu, wd, tm):
    p, half = xg.shape
    d = 2 * half
    dff = wg.shape[2]
    tf = _divisor_tile(dff, 512, LANE)
    nf = dff // tf
    nb = p // tm

    def row_map(i, f, be, nu):
        return (jnp.minimum(i, nu[0] - 1), 0)

    def f_eff(i, f, nu):
        return jnp.where(i < nu[0], f, nf - 1)

    grid_spec = pltpu.PrefetchScalarGridSpec(
        num_scalar_prefetch=2,
        grid=(nb, nf),
        in_specs=[
            pl.BlockSpec((tm, half), row_map),
            pl.BlockSpec((1, d, tf), lambda i, f, be, nu: (be[i], 0, f_eff(i, f, nu))),
            pl.BlockSpec((1, d, tf), lambda i, f, be, nu: (be[i], 0, f_eff(i, f, nu))),
            pl.BlockSpec((1, tf, d), lambda i, f, be, nu: (be[i], f_eff(i, f, nu), 0)),
        ],
        out_specs=pl.BlockSpec((tm, d), lambda i, f, be, nu: (i, 0)),
        scratch_shapes=[pltpu.VMEM((tm, d), BF16)],
    )
    return pl.pallas_call(
        _moe_ffn_kernel,
        grid_spec=grid_spec,
        out_shape=jax.ShapeDtypeStruct((p, d), F32),
        compiler_params=_cparams(("arbitrary", "arbitrary")),
        name="moe_ffn",
    )(blk_e, nblk, xg, wg, wu, wd)


def _combine_kernel(d_ref, route_ref, h_ref, y_hbm, o_ref, y0_ref, y1_ref, sem, *, rows, lp, npad, nbatch):
    def start(r, carry):
        _row_copy(y_hbm, d_ref[0, 0, 2 * r], y0_ref, r, sem).start(priority=0)
        _row_copy(y_hbm, d_ref[0, 0, 2 * r + 1], y1_ref, r, sem).start(priority=1)
        return carry

    lax.fori_loop(0, rows, start, 0, unroll=4)
    pltpu.make_async_copy(y_hbm.at[pl.ds(0, rows)], y0_ref, sem).wait()
    pltpu.make_async_copy(y_hbm.at[pl.ds(0, rows)], y1_ref, sem).wait()

    g0 = route_ref[:, 2:3]
    g1 = route_ref[:, 3:4]
    y = g0 * y0_ref[...] + g1 * y1_ref[...]
    valid = _row_valid(pl.program_id(0) * rows, rows, lp, npad, nbatch)
    o_ref[...] = h_ref[...] + jnp.where(valid, y, 0.0)


def _moe_combine(h, route, dest, yg, rows, lp, npad, nbatch, drop_lead):
    n, d = h.shape
    nb = n // rows
    d3 = dest.reshape(nb, 1, 2 * rows)
    tpb = lp // rows
    if drop_lead:
        out_rows = n - nbatch * rows

        def out_map(i):
            return ((i // tpb) * (tpb - 1) + jnp.maximum(i % tpb - 1, 0), 0)
    else:
        out_rows = n

        def out_map(i):
            return (i, 0)

    return pl.pallas_call(
        functools.partial(_combine_kernel, rows=rows, lp=lp, npad=npad, nbatch=nbatch),
        grid=(nb,),
        in_specs=[
            pl.BlockSpec((1, 1, 2 * rows), lambda i: (i, 0, 0), memory_space=pltpu.SMEM),
            pl.BlockSpec((rows, LANE), lambda i: (i, 0)),
            pl.BlockSpec((rows, d), lambda i: (i, 0)),
            pl.BlockSpec(memory_space=pl.ANY),
        ],
        out_specs=pl.BlockSpec((rows, d), out_map),
        out_shape=jax.ShapeDtypeStruct((out_rows, d), F32),
        scratch_shapes=[pltpu.VMEM((rows, d), F32), pltpu.VMEM((rows, d), F32), pltpu.SemaphoreType.DMA(())],
        compiler_params=_cparams(("arbitrary",)),
        name="moe_combine",
    )(d3, route, h, yg)


def _moe_layout(experts, tm, nblocks):
    n = experts.shape[0]
    flat_e = experts.reshape(-1)
    onehot = (flat_e[:, None] == jnp.arange(N_EXPERTS, dtype=jnp.int32)[None, :]).astype(jnp.int32)
    csum = jnp.cumsum(onehot, axis=0)
    pos = jnp.take_along_axis(csum, flat_e[:, None], axis=1)[:, 0] - 1
    counts = csum[-1]
    padded = (counts + tm - 1) // tm * tm
    ends = jnp.cumsum(padded)
    gstart = ends - padded
    dest = (gstart[flat_e] + pos).astype(jnp.int32)
    flat_tok = jnp.arange(2 * n, dtype=jnp.int32) // 2
    slot_tok = jnp.zeros((nblocks * tm,), jnp.int32).at[dest].set(flat_tok)
    nblk = (ends[-1] // tm).astype(jnp.int32)
    blk = jnp.minimum(jnp.arange(nblocks, dtype=jnp.int32), nblk - 1)
    blk_e = jnp.minimum(jnp.searchsorted(ends, blk * tm, side="right"), N_EXPERTS - 1).astype(jnp.int32)
    return dest, slot_tok, blk_e, nblk.reshape(1)


def _moe_block(h, gain, rw, wg, wu, wd, lp, npad, nbatch, lead_rows):
    n, d = h.shape
    xp, route = _router(h, gain, rw)
    experts = route[:, :2].astype(jnp.int32)
    tm = _divisor_tile(n, 1024, GATHER_ROWS) if n % GATHER_ROWS == 0 else n
    rows = min(GATHER_ROWS, tm)
    nblocks = -(-(2 * n + N_EXPERTS * (tm - 1)) // tm)
    dest, slot_tok, blk_e, nblk = _moe_layout(experts, tm, nblocks)
    xg = _moe_gather(xp, slot_tok, rows)
    yg = _moe_ffn(xg, blk_e, nblk, wg, wu, wd, tm)
    return _moe_combine(h, route, dest, yg, rows, lp, npad, nbatch, drop_lead=(lead_rows == rows))


def _inproj_layout(d_model):
    att_w = HEADS * ATT_DH
    qk_w = HEADS * ML_DK
    v_w = HEADS * ML_DV
    names = ("qa", "ka", "va", "fa", "qm", "km", "vm", "im", "fm", "om", "ga", "gm")
    widths = (att_w, att_w, att_w, HEADS, qk_w, qk_w, v_w, HEADS, HEADS, v_w, d_model, d_model)
    src, off = {}, 0
    for nm, wd in zip(names, widths):
        src[nm] = (off, wd)
        off += wd
    big_order = ("vm", "om", "ga", "gm", "qa", "ka", "va", "qm", "km")
    cols, off = {}, 0
    for nm in big_order:
        cols[nm] = off
        off += src[nm][1]
    return src, big_order, cols


def kernel(x, meta_tokens, ln_mix, w_in, att_q_norm, att_k_norm, att_f_bias, ml_conv_w, ml_conv_b, ml_i_bias,
           ml_f_bias, ml_h_norm, w_branch_att, w_branch_ml, w_out, ln_ffn, dense_w_gate, dense_w_up, dense_w_down,
           router_w, moe_w_gate, moe_w_up, moe_w_down):
    nbatch, seq, d = x.shape
    depth = w_in.shape[0]
    lp = -(-(CHUNK + seq) // SEQ_TILE) * SEQ_TILE
    npad = lp - seq - N_META
    n = nbatch * lp
    src, big_order, cols = _inproj_layout(d)

    h = jnp.concatenate([
        jnp.zeros((nbatch, npad, d), x.dtype),
        jnp.broadcast_to(meta_tokens[None].astype(x.dtype), (nbatch, N_META, d)),
        x], axis=1).reshape(n, d)

    for layer in range(depth):
        wl = w_in[layer]
        w_big = jnp.concatenate([wl[:, src[nm][0]:src[nm][0] + src[nm][1]] for nm in big_order], axis=1).astype(BF16)
        w_gates = jnp.concatenate(
            [wl[:, src[nm][0]:src[nm][0] + HEADS] for nm in ("fa", "fm", "im")]
            + [jnp.zeros((d, LANE - 3 * HEADS), F32)], axis=1)
        gate_bias = jnp.concatenate([att_f_bias[layer], ml_f_bias[layer], ml_i_bias[layer],
                                     jnp.zeros((LANE - 3 * HEADS,), F32)]).reshape(1, LANE)

        proj, gates = _norm_inproj(h, ln_mix[layer].reshape(1, d), w_big, w_gates)
        proj3 = proj.reshape(nbatch, lp, -1)
        gcol, grow = _gate_prep(gates.reshape(nbatch, lp, LANE), gate_bias, npad)
        qa, ka, qm, kmt = _qk_prep(proj3, cols, att_q_norm[layer].reshape(1, ATT_DH),
                                   att_k_norm[layer].reshape(1, ATT_DH), ml_conv_w[layer],
                                   ml_conv_b[layer].reshape(1, -1), npad)
        c_rows = grow[:, :HEADS, :].reshape(nbatch * HEADS, lp // SEQ_TILE, 1, SEQ_TILE)
        y_att = _fox_attention(qa, ka, proj3, cols["va"], c_rows, npad)
        y_ml = _mlstm(qm, kmt, proj3, cols, gcol, grow, ml_h_norm[layer].reshape(1, -1))
        merged = _merge(y_att.reshape(n, -1), y_ml.reshape(n, -1), w_branch_att[layer].astype(BF16),
                        w_branch_ml[layer].astype(BF16), proj, cols)
        h = _out_residual(merged, w_out[layer].astype(BF16), h, lp, npad, nbatch)

        j = layer // 2
        gain = ln_ffn[layer].reshape(1, d)
        if layer % 2 == 0:
            h = _ffn(h, gain, dense_w_gate[j].astype(BF16), dense_w_up[j].astype(BF16),
                     dense_w_down[j].astype(BF16), lp, npad, nbatch)
        else:
            rw = jnp.concatenate([router_w[j], jnp.zeros((d, LANE - N_EXPERTS), F32)], axis=1)
            h = _moe_block(h, gain, rw, moe_w_gate[j].astype(BF16), moe_w_up[j].astype(BF16),
                           moe_w_down[j].astype(BF16), lp, npad, nbatch,
                           lead_rows=(lp - seq) if layer == depth - 1 else 0)

    if h.shape[0] == nbatch * seq:
        return h.reshape(nbatch, seq, d)
    return h.reshape(nbatch, lp, d)[:, lp - seq:, :]
```

```python
import functools

import jax
import jax.numpy as jnp
from jax import lax
from jax.experimental import pallas as pl
from jax.experimental.pallas import tpu as pltpu

N_META = 16
CHUNK = 128
HEADS = 8
ATT_DH = 128
ML_DK = 128
ML_DV = 256
CONV_WIDTH = 4
N_EXPERTS = 8
EPS = 1e-6
NEG = -1e30
LOG2E = 1.4426950408889634

LANE = 128
SUBLANE = 8
VMEM_BYTES_V7X = 64 * 1024 * 1024
VMEM_LIMIT = 56 * 1024 * 1024

SEQ_TILE = 512
HALO = SUBLANE

F32 = jnp.float32
BF16 = jnp.bfloat16
HIGHEST = lax.Precision.HIGHEST


def _divisor_tile(n, pref, align):
    if n <= pref:
        return n
    t = (pref // align) * align
    while t >= align:
        if n % t == 0:
            return t
        t -= align
    return n


def _cparams(sem):
    return pltpu.CompilerParams(dimension_semantics=sem, vmem_limit_bytes=VMEM_LIMIT)


def _row_valid(row0, nrows, lp, npad, nbatch):
    r = row0 + lax.broadcasted_iota(jnp.int32, (nrows, 1), 0)
    invalid = (r >= 0) & (r < npad)
    for b in range(1, nbatch):
        invalid = invalid | ((r >= b * lp) & (r < b * lp + npad))
    return jnp.logical_not(invalid)


def _log_sigmoid(x):
    return jnp.minimum(x, 0.0) - jnp.log1p(jnp.exp(-jnp.abs(x)))


def _sigmoid(x):
    return 1.0 / (1.0 + jnp.exp(-x))


def _dot_f32_3pass(x, w):
    xh = x.astype(BF16)
    xl = (x - xh.astype(F32)).astype(BF16)
    wh = w.astype(BF16)
    wl = (w - wh.astype(F32)).astype(BF16)
    return (jnp.dot(xh, wh, preferred_element_type=F32) + jnp.dot(xl, wh, preferred_element_type=F32)
            + jnp.dot(xh, wl, preferred_element_type=F32))


def _norm_inproj_kernel(x_ref, g_ref, w_ref, wg_ref, o_ref, og_ref, xn_ref):
    @pl.when(pl.program_id(1) == 0)
    def _():
        x = x_ref[...]
        ms = jnp.mean(x * x, axis=-1, keepdims=True)
        xn = x * lax.rsqrt(ms + EPS) * g_ref[...]
        xn_ref[...] = xn.astype(BF16)
        og_ref[...] = _dot_f32_3pass(xn, wg_ref[...])

    o_ref[...] = jnp.dot(xn_ref[...], w_ref[...], preferred_element_type=F32).astype(o_ref.dtype)


def _norm_inproj(h, gain, w_big, w_gates):
    n, d = h.shape
    wn = w_big.shape[1]
    tm = _divisor_tile(n, 1024, SUBLANE)
    tn = _divisor_tile(wn, 1024, LANE)
    return pl.pallas_call(
        _norm_inproj_kernel,
        grid=(n // tm, wn // tn),
        in_specs=[
            pl.BlockSpec((tm, d), lambda i, j: (i, 0)),
            pl.BlockSpec((1, d), lambda i, j: (0, 0)),
            pl.BlockSpec((d, tn), lambda i, j: (0, j)),
            pl.BlockSpec((d, LANE), lambda i, j: (0, 0)),
        ],
        out_specs=[
            pl.BlockSpec((tm, tn), lambda i, j: (i, j)),
            pl.BlockSpec((tm, LANE), lambda i, j: (i, 0)),
        ],
        out_shape=[jax.ShapeDtypeStruct((n, wn), BF16), jax.ShapeDtypeStruct((n, LANE), F32)],
        scratch_shapes=[pltpu.VMEM((tm, d), BF16)],
        compiler_params=_cparams(("parallel", "arbitrary")),
        name="norm_inproj",
    )(h, gain, w_big, w_gates)


def _gate_prep_kernel(g_ref, b_ref, col_ref, row_ref, carry_ref, *, npad):
    c = pl.program_id(1)

    @pl.when(c == 0)
    def _():
        carry_ref[...] = jnp.zeros_like(carry_ref)

    lane = lax.broadcasted_iota(jnp.int32, (CHUNK, LANE), 1)
    r_i = lax.broadcasted_iota(jnp.int32, (CHUNK, CHUNK), 0)
    c_i = lax.broadcasted_iota(jnp.int32, (CHUNK, CHUNK), 1)
    tril = (r_i >= c_i).astype(F32)
    carry = carry_ref[0:1, :]
    for u in range(g_ref.shape[1] // CHUNK):
        rows = slice(u * CHUNK, (u + 1) * CHUNK)
        x = g_ref[0, rows, :] + b_ref[...]
        pos = c * g_ref.shape[1] + u * CHUNK + lax.broadcasted_iota(jnp.int32, (CHUNK, LANE), 0)
        valid = pos >= npad
        ls = _log_sigmoid(x)
        z = jnp.where(lane < HEADS, ls, jnp.where((lane < 2 * HEADS) & valid, ls, 0.0))
        s = jnp.dot(tril, z, preferred_element_type=F32, precision=HIGHEST)
        s = s + jnp.where(lane < HEADS, carry, 0.0)
        carry = s[CHUNK - 1:CHUNK, :]
        ipre = jnp.where(valid, x, -jnp.inf)
        col = jnp.where(lane < HEADS, s * LOG2E,
                        jnp.where(lane < 2 * HEADS, s, jnp.where(lane < 3 * HEADS, ipre, 0.0)))
        col_ref[0, rows, :] = col
        row_ref[0, :, rows] = col.T[:3 * HEADS, :]
    carry_ref[...] = jnp.broadcast_to(carry, carry_ref.shape)


def _gate_prep(gates3, bias, npad):
    b, lp, _ = gates3.shape
    rows = SEQ_TILE
    nc = lp // rows
    return pl.pallas_call(
        functools.partial(_gate_prep_kernel, npad=npad),
        grid=(b, nc),
        in_specs=[
            pl.BlockSpec((1, rows, LANE), lambda bi, c: (bi, c, 0)),
            pl.BlockSpec((1, LANE), lambda bi, c: (0, 0)),
        ],
        out_specs=[
            pl.BlockSpec((1, rows, LANE), lambda bi, c: (bi, c, 0)),
            pl.BlockSpec((1, 3 * HEADS, rows), lambda bi, c: (bi, 0, c)),
        ],
        out_shape=[jax.ShapeDtypeStruct((b, lp, LANE), F32), jax.ShapeDtypeStruct((b, 3 * HEADS, lp), F32)],
        scratch_shapes=[pltpu.VMEM((SUBLANE, LANE), F32)],
        compiler_params=_cparams(("parallel", "arbitrary")),
        name="gate_prep",
    )(gates3, bias)


def _qk_prep_kernel(qa_ref, ka_ref, qm_ref, km_ref, qmh_ref, kmh_ref, gq_ref, gk_ref, cw_ref, cb_ref,
                    qa_o, ka_o, qm_o, kmt_o, win_ref, *, npad):
    c = pl.program_id(1)
    ml_w = HEADS * ML_DK

    for hh in range(HEADS):
        sl = slice(hh * ATT_DH, (hh + 1) * ATT_DH)
        q = qa_ref[0, :, sl].astype(F32)
        q = q * lax.rsqrt(jnp.mean(q * q, axis=-1, keepdims=True) + EPS) * gq_ref[...]
        qa_o[0, :, sl] = (q * (ATT_DH ** -0.5 * LOG2E)).astype(BF16)
        k = ka_ref[0, :, sl].astype(F32)
        k = k * lax.rsqrt(jnp.mean(k * k, axis=-1, keepdims=True) + EPS) * gk_ref[...]
        ka_o[0, :, sl] = k.astype(BF16)

    pos = c * CHUNK + lax.broadcasted_iota(jnp.int32, (CHUNK, 1), 0)
    hpos = c * CHUNK - HALO + lax.broadcasted_iota(jnp.int32, (HALO, 1), 0)
    row_ok = pos >= npad
    halo_ok = (hpos >= npad) & (c > 0)

    r_i = lax.broadcasted_iota(jnp.int32, (CHUNK, CHUNK), 0)
    c_i = lax.broadcasted_iota(jnp.int32, (CHUNK, CHUNK), 1)
    shift = [(r_i - c_i == CONV_WIDTH - 1 - j).astype(BF16) for j in range(CONV_WIDTH - 1)]

    def conv_silu(x_ref, xh_ref, col0):
        def tap(j):
            return cw_ref[j:j + 1, col0:col0 + ml_w]

        bias = cb_ref[:, col0:col0 + ml_w]
        xf = jnp.where(row_ok, x_ref[0].astype(F32), 0.0)
        xm = xf.astype(BF16)
        y = bias + tap(CONV_WIDTH - 1) * xf
        for j in range(CONV_WIDTH - 1):
            y = y + tap(j) * jnp.dot(shift[j], xm, preferred_element_type=F32)
        win_ref[0:HALO, :] = jnp.where(halo_ok, xh_ref[0].astype(F32), 0.0)
        win_ref[HALO:2 * HALO, :] = xf[0:HALO, :]
        y_head = jnp.broadcast_to(bias, (HALO, ml_w))
        for j in range(CONV_WIDTH):
            off = HALO - (CONV_WIDTH - 1) + j
            y_head = y_head + tap(j) * win_ref[off:off + HALO, :]
        y = jnp.concatenate([y_head, y[HALO:, :]], axis=0)
        return y * _sigmoid(y)

    yq = conv_silu(qm_ref, qmh_ref, 0)
    qm_o[0] = yq.astype(BF16)
    yk = conv_silu(km_ref, kmh_ref, ml_w) * (ML_DK ** -0.5)
    for hh in range(HEADS):
        sl = slice(hh * ML_DK, (hh + 1) * ML_DK)
        kmt_o[0, sl, :] = yk[:, sl].T.astype(BF16)


def _qk_prep(proj3, cols, gq, gk, conv_w, conv_b, npad):
    b, lp, _ = proj3.shape
    nc = lp // CHUNK
    w = HEADS * ATT_DH
    qa_c, ka_c, qm_c, km_c = (cols[k] // w for k in ("qa", "ka", "qm", "km"))
    hpc = CHUNK // HALO

    def main(cb):
        return pl.BlockSpec((1, CHUNK, w), lambda bi, c: (bi, c, cb))

    def halo(cb):
        return pl.BlockSpec((1, HALO, w), lambda bi, c: (bi, jnp.maximum(c * hpc - 1, 0), cb))

    full = lambda shape: pl.BlockSpec(shape, lambda bi, c: (0,) * len(shape))
    return pl.pallas_call(
        functools.partial(_qk_prep_kernel, npad=npad),
        grid=(b, nc),
        in_specs=[main(qa_c), main(ka_c), main(qm_c), main(km_c), halo(qm_c), halo(km_c),
                  full((1, ATT_DH)), full((1, ATT_DH)), full((CONV_WIDTH, 2 * w)), full((1, 2 * w))],
        out_specs=[
            pl.BlockSpec((1, CHUNK, w), lambda bi, c: (bi, c, 0)),
            pl.BlockSpec((1, CHUNK, w), lambda bi, c: (bi, c, 0)),
            pl.BlockSpec((1, CHUNK, w), lambda bi, c: (bi, c, 0)),
            pl.BlockSpec((1, w, CHUNK), lambda bi, c: (bi, 0, c)),
        ],
        out_shape=[jax.ShapeDtypeStruct((b, lp, w), BF16)] * 3 + [jax.ShapeDtypeStruct((b, w, lp), BF16)],
        scratch_shapes=[pltpu.VMEM((2 * HALO, w), F32)],
        compiler_params=_cparams(("parallel", "arbitrary")),
        name="qk_prep",
    )(proj3, proj3, proj3, proj3, proj3, proj3, gq, gk, conv_w, conv_b)


FOX_PART_ROWS = 512
FOX_Q_TILE_MAX = 1536


def _fox_kernel(q_ref, k_ref, v_ref, c_ref, o_ref, m_ref, acc_ref, s0_ref, s1_ref, *, tq, tk, npad):
    i = pl.program_id(2)
    r = tq // tk
    hq = min(FOX_PART_ROWS, tk)
    slots = (s0_ref, s1_ref)
    c0 = c_ref[0, r * i][:, 0:1]
    m_ref[...] = jnp.full_like(m_ref, NEG)
    acc_ref[...] = jnp.zeros_like(acc_ref)
    ones = jnp.ones((tk, LANE), BF16)

    def kv_start(j):
        return j * tk if isinstance(j, int) else pl.multiple_of(j * tk, tk)

    def scores(j, s_ref, masked, row0):
        k = k_ref[0, pl.ds(kv_start(j), tk), :]
        s = lax.dot_general(q_ref[0, row0:, :], k, (((1,), (1,)), ((), ())), preferred_element_type=F32)
        s = s + (c0 - c_ref[0, j])
        if masked:
            qpos = i * tq + row0 + lax.broadcasted_iota(jnp.int32, (tq - row0, tk), 0)
            kpos = j * tk + lax.broadcasted_iota(jnp.int32, (tq - row0, tk), 1)
            s = jnp.where((kpos <= qpos) & (kpos >= npad), s, NEG)
        s_ref[row0:, :] = s

    def accumulate(j, s_ref, row0):
        vext = jnp.concatenate([v_ref[0, pl.ds(kv_start(j), tk), :], ones], axis=1)
        for part in range(row0 // hq, tq // hq):
            rows = slice(part * hq, (part + 1) * hq)
            s = s_ref[rows, :]
            m_prev = m_ref[rows, :]
            m_new = jnp.maximum(m_prev, jnp.broadcast_to(jnp.max(s, axis=1, keepdims=True), (hq, LANE)))
            alpha = jnp.exp2(m_prev - m_new)
            p = jnp.exp2(s - jnp.concatenate([m_new] * (tk // LANE), axis=1))
            pv = jnp.dot(p.astype(BF16), vext, preferred_element_type=F32)
            acc_ref[rows, :] = jnp.concatenate([alpha, alpha], axis=1) * acc_ref[rows, :] + pv
            m_ref[rows, :] = m_new

    def run(chunks):
        for n, (j, d) in enumerate(chunks):
            if n + 1 < len(chunks):
                jn, dn = chunks[n + 1]
                scores(jn, slots[(n + 1) % 2], dn is not None, 0 if dn is None else dn * tk)
            accumulate(j, slots[n % 2], 0 if d is None else d * tk)

    first = r * i
    diag = [(first + d, d) for d in range(r)]
    scores(0, s0_ref, True, 0)

    n_plain = jnp.maximum(first - 1, 0)
    odd = lax.bitwise_and(n_plain, 1)

    def pair_body(p, carry):
        j = 2 * p
        scores(j + 1, s1_ref, False, 0)
        accumulate(j, s0_ref, 0)
        scores(j + 2, s0_ref, False, 0)
        accumulate(j + 1, s1_ref, 0)
        return carry

    lax.fori_loop(0, lax.shift_right_logical(n_plain, 1), pair_body, 0)

    @pl.when(i == 0)
    def _():
        run(diag)

    @pl.when((i > 0) & (odd == 0))
    def _():
        run([(first - 1, None)] + diag)

    @pl.when((i > 0) & (odd == 1))
    def _():
        run([(first - 2, None), (first - 1, None)] + diag)

    o_ref[0] = (acc_ref[:, :ATT_DH] / acc_ref[:, ATT_DH:]).astype(o_ref.dtype)


def _fox_attention(qa, ka, proj3, v_col, c_rows, npad):
    b, lp, w = qa.shape
    tk = SEQ_TILE
    tq = _divisor_tile(lp, FOX_Q_TILE_MAX, tk)
    nq = lp // tq
    nk = lp // tk
    vcb = v_col // ATT_DH
    return pl.pallas_call(
        functools.partial(_fox_kernel, tq=tq, tk=tk, npad=npad),
        grid=(b, HEADS, nq),
        in_specs=[
            pl.BlockSpec((1, tq, ATT_DH), lambda bi, h, i: (bi, i, h)),
            pl.BlockSpec((1, lp, ATT_DH), lambda bi, h, i: (bi, 0, h)),
            pl.BlockSpec((1, lp, ATT_DH), lambda bi, h, i: (bi, 0, vcb + h)),
            pl.BlockSpec((1, nk, 1, tk), lambda bi, h, i: (bi * HEADS + h, 0, 0, 0)),
        ],
        out_specs=pl.BlockSpec((1, tq, ATT_DH), lambda bi, h, i: (bi, i, h)),
        out_shape=jax.ShapeDtypeStruct((b, lp, w), BF16),
        scratch_shapes=[pltpu.VMEM((tq, LANE), F32), pltpu.VMEM((tq, ATT_DH + LANE), F32),
                        pltpu.VMEM((tq, tk), F32), pltpu.VMEM((tq, tk), F32)],
        compiler_params=_cparams(("parallel", "parallel", "arbitrary")),
        name="fox_attention",
    )(qa, ka, proj3, c_rows)


def _mlstm_kernel(q_ref, kt_ref, v_ref, om_ref, col_ref, row_ref, hn_ref, o_ref, ct_ref, m_ref):
    c = pl.program_id(1)

    @pl.when(c == 0)
    def _():
        ct_ref[...] = jnp.zeros_like(ct_ref)
        m_ref[...] = jnp.zeros_like(m_ref)

    t = CHUNK
    r_i = lax.broadcasted_iota(jnp.int32, (t, t), 0)
    c_i = lax.broadcasted_iota(jnp.int32, (t, t), 1)
    causal = r_i >= c_i
    ones = jnp.ones((t, LANE), BF16)

    for hh in range(HEADS):
        q = q_ref[0, :, hh * ML_DK:(hh + 1) * ML_DK]
        kt = kt_ref[0, hh * ML_DK:(hh + 1) * ML_DK, :]
        v = v_ref[0, :, hh * ML_DV:(hh + 1) * ML_DV]
        vext = jnp.concatenate([v, ones], axis=1)
        bt_c = col_ref[0, :, HEADS + hh:HEADS + hh + 1]
        bt_r = row_ref[0, HEADS + hh:HEADS + hh + 1, :]
        it_r = row_ref[0, 2 * HEADS + hh:2 * HEADS + hh + 1, :]
        g = bt_r[:, t - 1:t]
        m = m_ref[hh, 0:1, 0:1]

        dmat = jnp.where(causal, bt_c - bt_r + it_r, -jnp.inf)
        inter = bt_c + m
        m_row = jnp.maximum(inter, jnp.max(dmat, axis=1, keepdims=True))
        w_intra = jnp.exp(dmat - m_row)
        w_inter = jnp.exp(inter - m_row)
        qk = jnp.dot(q, kt, preferred_element_type=F32)
        a = (qk * w_intra).astype(BF16)
        ct = ct_ref[hh]
        nd = (jnp.dot(a, vext, preferred_element_type=F32)
              + w_inter * jnp.dot(q, ct.astype(BF16), preferred_element_type=F32))
        num = nd[:, :ML_DV]
        den = nd[:, ML_DV:ML_DV + 1]
        denom = jnp.maximum(jnp.abs(den), jnp.exp(-m_row))
        hv = num / denom

        src = g - bt_r + it_r
        m_new = jnp.maximum(g + m, jnp.max(src, axis=1, keepdims=True))
        w_src = jnp.exp(src - m_new)
        decay = jnp.exp(g + m - m_new)
        ks = (kt.astype(F32) * w_src).astype(BF16)
        ct_ref[hh] = decay * ct + jnp.dot(ks, vext, preferred_element_type=F32)
        m_ref[hh] = jnp.broadcast_to(m_new, (SUBLANE, LANE))

        hv = hv * lax.rsqrt(jnp.mean(hv * hv, axis=-1, keepdims=True) + EPS)
        hv = hv * hn_ref[:, hh * ML_DV:(hh + 1) * ML_DV]
        og = _sigmoid(om_ref[0, :, hh * ML_DV:(hh + 1) * ML_DV].astype(F32))
        o_ref[0, :, hh * ML_DV:(hh + 1) * ML_DV] = (og * hv).astype(o_ref.dtype)


def _mlstm(qm, kmt, proj3, cols, gcol, grow, hnorm):
    b, lp, qw = qm.shape
    nc = lp // CHUNK
    vw = HEADS * ML_DV
    v_cb = cols["vm"] // vw
    om_cb = cols["om"] // vw
    return pl.pallas_call(
        _mlstm_kernel,
        grid=(b, nc),
        in_specs=[
            pl.BlockSpec((1, CHUNK, qw), lambda bi, c: (bi, c, 0)),
            pl.BlockSpec((1, qw, CHUNK), lambda bi, c: (bi, 0, c)),
            pl.BlockSpec((1, CHUNK, vw), lambda bi, c: (bi, c, v_cb)),
            pl.BlockSpec((1, CHUNK, vw), lambda bi, c: (bi, c, om_cb)),
            pl.BlockSpec((1, CHUNK, LANE), lambda bi, c: (bi, c, 0)),
            pl.BlockSpec((1, 3 * HEADS, CHUNK), lambda bi, c: (bi, 0, c)),
            pl.BlockSpec((1, vw), lambda bi, c: (0, 0)),
        ],
        out_specs=pl.BlockSpec((1, CHUNK, vw), lambda bi, c: (bi, c, 0)),
        out_shape=jax.ShapeDtypeStruct((b, lp, vw), BF16),
        scratch_shapes=[pltpu.VMEM((HEADS, ML_DK, ML_DV + LANE), F32), pltpu.VMEM((HEADS, SUBLANE, LANE), F32)],
        compiler_params=_cparams(("parallel", "arbitrary")),
        name="mlstm",
    )(qm, kmt, proj3, proj3, gcol, grow, hnorm)


def _merge_kernel(ya_ref, ym_ref, wa_ref, wm_ref, ga_ref, gm_ref, o_ref):
    a = jnp.dot(ya_ref[...], wa_ref[...], preferred_element_type=F32)
    m = jnp.dot(ym_ref[...], wm_ref[...], preferred_element_type=F32)
    out = _sigmoid(ga_ref[...].astype(F32)) * a + _sigmoid(gm_ref[...].astype(F32)) * m
    o_ref[...] = out.astype(o_ref.dtype)


def _merge(y_att, y_ml, wa, wm, proj, cols):
    n, ka = y_att.shape
    km = y_ml.shape[1]
    d = wa.shape[1]
    tm = _divisor_tile(n, 1024, SUBLANE)
    tn = _divisor_tile(d, 1024, LANE)
    ga_cb = cols["ga"] // tn
    gm_cb = cols["gm"] // tn
    return pl.pallas_call(
        _merge_kernel,
        grid=(n // tm, d // tn),
        in_specs=[
            pl.BlockSpec((tm, ka), lambda i, j: (i, 0)),
            pl.BlockSpec((tm, km), lambda i, j: (i, 0)),
            pl.BlockSpec((ka, tn), lambda i, j: (0, j)),
            pl.BlockSpec((km, tn), lambda i, j: (0, j)),
            pl.BlockSpec((tm, tn), lambda i, j: (i, ga_cb + j)),
            pl.BlockSpec((tm, tn), lambda i, j: (i, gm_cb + j)),
        ],
        out_specs=pl.BlockSpec((tm, tn), lambda i, j: (i, j)),
        out_shape=jax.ShapeDtypeStruct((n, d), BF16),
        compiler_params=_cparams(("parallel", "arbitrary")),
        name="merge",
    )(y_att, y_ml, wa, wm, proj, proj)


def _out_residual_kernel(x_ref, w_ref, h_ref, o_ref, *, tm, lp, npad, nbatch):
    mix = jnp.dot(x_ref[...], w_ref[...], preferred_element_type=F32)
    valid = _row_valid(pl.program_id(0) * tm, tm, lp, npad, nbatch)
    o_ref[...] = h_ref[...] + jnp.where(valid, mix, 0.0)


def _out_residual(x, w, h, lp, npad, nbatch):
    n, k = x.shape
    d = w.shape[1]
    tm = _divisor_tile(n, 512, SUBLANE)
    tn = _divisor_tile(d, 2048, LANE)
    return pl.pallas_call(
        functools.partial(_out_residual_kernel, tm=tm, lp=lp, npad=npad, nbatch=nbatch),
        grid=(n // tm, d // tn),
        in_specs=[
            pl.BlockSpec((tm, k), lambda i, j: (i, 0)),
            pl.BlockSpec((k, tn), lambda i, j: (0, j)),
            pl.BlockSpec((tm, tn), lambda i, j: (i, j)),
        ],
        out_specs=pl.BlockSpec((tm, tn), lambda i, j: (i, j)),
        out_shape=jax.ShapeDtypeStruct((n, d), F32),
        compiler_params=_cparams(("parallel", "arbitrary")),
        name="out_residual",
    )(x, w, h)


def _ffn_kernel(x_ref, g_ref, wg_ref, wu_ref, wd_ref, o_ref, xn_ref, *, tm, lp, npad, nbatch):
    f = pl.program_id(1)

    @pl.when(f == 0)
    def _():
        x = x_ref[...]
        ms = jnp.mean(x * x, axis=-1, keepdims=True)
        xn_ref[...] = (x * lax.rsqrt(ms + EPS) * g_ref[...]).astype(BF16)
        o_ref[...] = jnp.zeros_like(o_ref)

    xn = xn_ref[...]
    gate = jnp.dot(xn, wg_ref[...], preferred_element_type=F32)
    up = jnp.dot(xn, wu_ref[...], preferred_element_type=F32)
    act = (gate * _sigmoid(gate) * up).astype(BF16)
    o_ref[...] += jnp.dot(act, wd_ref[...], preferred_element_type=F32)

    @pl.when(f == pl.num_programs(1) - 1)
    def _():
        valid = _row_valid(pl.program_id(0) * tm, tm, lp, npad, nbatch)
        o_ref[...] = x_ref[...] + jnp.where(valid, o_ref[...], 0.0)


def _ffn(h, gain, wg, wu, wd, lp, npad, nbatch):
    n, d = h.shape
    dff = wg.shape[1]
    tm = _divisor_tile(n, 1024, SUBLANE)
    tf = _divisor_tile(dff, 512, LANE)
    return pl.pallas_call(
        functools.partial(_ffn_kernel, tm=tm, lp=lp, npad=npad, nbatch=nbatch),
        grid=(n // tm, dff // tf),
        in_specs=[
            pl.BlockSpec((tm, d), lambda i, f: (i, 0)),
            pl.BlockSpec((1, d), lambda i, f: (0, 0)),
            pl.BlockSpec((d, tf), lambda i, f: (0, f)),
            pl.BlockSpec((d, tf), lambda i, f: (0, f)),
            pl.BlockSpec((tf, d), lambda i, f: (f, 0)),
        ],
        out_specs=pl.BlockSpec((tm, d), lambda i, f: (i, 0)),
        out_shape=jax.ShapeDtypeStruct((n, d), F32),
        scratch_shapes=[pltpu.VMEM((tm, d), BF16)],
        compiler_params=_cparams(("parallel", "arbitrary")),
        name="ffn",
    )(h, gain, wg, wu, wd)


def _bf16_bits(x):
    bits = lax.bitcast_convert_type(x, jnp.uint32)
    rounded = bits + jnp.uint32(0x7FFF) + ((bits >> 16) & jnp.uint32(1))
    return rounded >> 16


def _router_kernel(x_ref, g_ref, rw_ref, xp_ref, route_ref):
    x = x_ref[...]
    d = x.shape[1]
    ms = jnp.mean(x * x, axis=-1, keepdims=True)
    xn = x * lax.rsqrt(ms + EPS) * g_ref[...]
    xp_ref[...] = _bf16_bits(xn[:, :d // 2]) | (_bf16_bits(xn[:, d // 2:]) << 16)

    logits = _dot_f32_3pass(xn, rw_ref[...])
    lane = lax.broadcasted_iota(jnp.int32, logits.shape, 1)
    logits = jnp.where(lane < N_EXPERTS, logits, -jnp.inf)
    v1 = jnp.max(logits, axis=-1, keepdims=True)
    i1 = jnp.min(jnp.where(logits == v1, lane, LANE), axis=-1, keepdims=True)
    rest = jnp.where(lane == i1, -jnp.inf, logits)
    v2 = jnp.max(rest, axis=-1, keepdims=True)
    i2 = jnp.min(jnp.where(rest == v2, lane, LANE), axis=-1, keepdims=True)
    e = jnp.exp(v2 - v1)
    g1 = 1.0 / (1.0 + e)
    g2 = e / (1.0 + e)
    route = jnp.where(lane == 0, i1.astype(F32),
                      jnp.where(lane == 1, i2.astype(F32),
                                jnp.where(lane == 2, g1, jnp.where(lane == 3, g2, 0.0))))
    route_ref[...] = route


def _router(h, gain, rw):
    n, d = h.shape
    tm = _divisor_tile(n, 1024, SUBLANE)
    return pl.pallas_call(
        _router_kernel,
        grid=(n // tm,),
        in_specs=[
            pl.BlockSpec((tm, d), lambda i: (i, 0)),
            pl.BlockSpec((1, d), lambda i: (0, 0)),
            pl.BlockSpec((d, LANE), lambda i: (0, 0)),
        ],
        out_specs=[
            pl.BlockSpec((tm, d // 2), lambda i: (i, 0)),
            pl.BlockSpec((tm, LANE), lambda i: (i, 0)),
        ],
        out_shape=[jax.ShapeDtypeStruct((n, d // 2), jnp.uint32), jax.ShapeDtypeStruct((n, LANE), F32)],
        compiler_params=_cparams(("parallel",)),
        name="router",
    )(h, gain, rw)


GATHER_ROWS = 512


def _row_copy(src_hbm, src_row, dst_ref, dst_row, sem):
    return pltpu.make_async_copy(src_hbm.at[pl.ds(src_row, 1)], dst_ref.at[pl.ds(dst_row, 1)], sem)


def _gather_kernel(idx_ref, x_hbm, o_ref, sem, *, rows):
    def start(r2, carry):
        for u in range(2):
            r = 2 * r2 + u
            _row_copy(x_hbm, idx_ref[0, 0, r], o_ref, r, sem).start(priority=u)
        return carry

    lax.fori_loop(0, rows // 2, start, 0, unroll=4)
    pltpu.make_async_copy(x_hbm.at[pl.ds(0, rows)], o_ref, sem).wait()


def _moe_gather(xp, slot_tok, rows):
    p = slot_tok.shape[0]
    w = xp.shape[1]
    nb = p // rows
    idx3 = slot_tok.reshape(nb, 1, rows)
    return pl.pallas_call(
        functools.partial(_gather_kernel, rows=rows),
        grid=(nb,),
        in_specs=[
            pl.BlockSpec((1, 1, rows), lambda i: (i, 0, 0), memory_space=pltpu.SMEM),
            pl.BlockSpec(memory_space=pl.ANY),
        ],
        out_specs=pl.BlockSpec((rows, w), lambda i: (i, 0)),
        out_shape=jax.ShapeDtypeStruct((p, w), xp.dtype),
        scratch_shapes=[pltpu.SemaphoreType.DMA(())],
        compiler_params=_cparams(("arbitrary",)),
        name="moe_gather",
    )(idx3, xp)


def _moe_ffn_kernel(blk_e_ref, nblk_ref, xg_ref, wg_ref, wu_ref, wd_ref, o_ref, xn_ref):
    i = pl.program_id(0)
    f = pl.program_id(1)

    @pl.when(i < nblk_ref[0])
    def _():
        @pl.when(f == 0)
        def _():
            packed = xg_ref[...]
            half = packed.shape[1]
            lo = lax.bitcast_convert_type(packed << 16, F32)
            hi = lax.bitcast_convert_type(packed & jnp.uint32(0xFFFF0000), F32)
            xn_ref[:, :half] = lo.astype(BF16)
            xn_ref[:, half:] = hi.astype(BF16)
            o_ref[...] = jnp.zeros_like(o_ref)

        xn = xn_ref[...]
        gate = jnp.dot(xn, wg_ref[0], preferred_element_type=F32)
        up = jnp.dot(xn, wu_ref[0], preferred_element_type=F32)
        act = (gate * _sigmoid(gate) * up).astype(BF16)
        o_ref[...] += jnp.dot(act, wd_ref[0], preferred_element_type=F32)

    @pl.when((i >= nblk_ref[0]) & (f == 0))
    def _():
        o_ref[...] = jnp.zeros_like(o_ref)


def _moe_ffn(xg, blk_e, nblk, wg, wu, wd, tm):
    p, half = xg.shape
    d = 2 * half
    dff = wg.shape[2]
    tf = _divisor_tile(dff, 512, LANE)
    nf = dff // tf
    nb = p // tm

    def row_map(i, f, be, nu):
        return (jnp.minimum(i, nu[0] - 1), 0)

    def f_eff(i, f, nu):
        return jnp.where(i < nu[0], f, nf - 1)

    grid_spec = pltpu.PrefetchScalarGridSpec(
        num_scalar_prefetch=2,
        grid=(nb, nf),
        in_specs=[
            pl.BlockSpec((tm, half), row_map),
            pl.BlockSpec((1, d, tf), lambda i, f, be, nu: (be[i], 0, f_eff(i, f, nu))),
            pl.BlockSpec((1, d, tf), lambda i, f, be, nu: (be[i], 0, f_eff(i, f, nu))),
            pl.BlockSpec((1, tf, d), lambda i, f, be, nu: (be[i], f_eff(i, f, nu), 0)),
        ],
        out_specs=pl.BlockSpec((tm, d), lambda i, f, be, nu: (i, 0)),
        scratch_shapes=[pltpu.VMEM((tm, d), BF16)],
    )
    return pl.pallas_call(
        _moe_ffn_kernel,
        grid_spec=grid_spec,
        out_shape=jax.ShapeDtypeStruct((p, d), F32),
        compiler_params=_cparams(("arbitrary", "arbitrary")),
        name="moe_ffn",
    )(blk_e, nblk, xg, wg, wu, wd)


def _combine_kernel(d_ref, route_ref, h_ref, y_hbm, o_ref, y0_ref, y1_ref, sem, *, rows, lp, npad, nbatch):
    def start(r, carry):
        _row_copy(y_hbm, d_ref[0, 0, 2 * r], y0_ref, r, sem).start(priority=0)
        _row_copy(y_hbm, d_ref[0, 0, 2 * r + 1], y1_ref, r, sem).start(priority=1)
        return carry

    lax.fori_loop(0, rows, start, 0, unroll=4)
    pltpu.make_async_copy(y_hbm.at[pl.ds(0, rows)], y0_ref, sem).wait()
    pltpu.make_async_copy(y_hbm.at[pl.ds(0, rows)], y1_ref, sem).wait()

    g0 = route_ref[:, 2:3]
    g1 = route_ref[:, 3:4]
    y = g0 * y0_ref[...] + g1 * y1_ref[...]
    valid = _row_valid(pl.program_id(0) * rows, rows, lp, npad, nbatch)
    o_ref[...] = h_ref[...] + jnp.where(valid, y, 0.0)


def _moe_combine(h, route, dest, yg, rows, lp, npad, nbatch, drop_lead):
    n, d = h.shape
    nb = n // rows
    d3 = dest.reshape(nb, 1, 2 * rows)
    tpb = lp // rows
    if drop_lead:
        out_rows = n - nbatch * rows

        def out_map(i):
            return ((i // tpb) * (tpb - 1) + jnp.maximum(i % tpb - 1, 0), 0)
    else:
        out_rows = n

        def out_map(i):
            return (i, 0)

    return pl.pallas_call(
        functools.partial(_combine_kernel, rows=rows, lp=lp, npad=npad, nbatch=nbatch),
        grid=(nb,),
        in_specs=[
            pl.BlockSpec((1, 1, 2 * rows), lambda i: (i, 0, 0), memory_space=pltpu.SMEM),
            pl.BlockSpec((rows, LANE), lambda i: (i, 0)),
            pl.BlockSpec((rows, d), lambda i: (i, 0)),
            pl.BlockSpec(memory_space=pl.ANY),
        ],
        out_specs=pl.BlockSpec((rows, d), out_map),
        out_shape=jax.ShapeDtypeStruct((out_rows, d), F32),
        scratch_shapes=[pltpu.VMEM((rows, d), F32), pltpu.VMEM((rows, d), F32), pltpu.SemaphoreType.DMA(())],
        compiler_params=_cparams(("arbitrary",)),
        name="moe_combine",
    )(d3, route, h, yg)


def _moe_layout(experts, tm, nblocks):
    n = experts.shape[0]
    flat_e = experts.reshape(-1)
    onehot = (flat_e[:, None] == jnp.arange(N_EXPERTS, dtype=jnp.int32)[None, :]).astype(jnp.int32)
    csum = jnp.cumsum(onehot, axis=0)
    pos = jnp.take_along_axis(csum, flat_e[:, None], axis=1)[:, 0] - 1
    counts = csum[-1]
    padded = (counts + tm - 1) // tm * tm
    ends = jnp.cumsum(padded)
    gstart = ends - padded
    dest = (gstart[flat_e] + pos).astype(jnp.int32)
    flat_tok = jnp.arange(2 * n, dtype=jnp.int32) // 2
    slot_tok = jnp.zeros((nblocks * tm,), jnp.int32).at[dest].set(flat_tok)
    nblk = (ends[-1] // tm).astype(jnp.int32)
    blk = jnp.minimum(jnp.arange(nblocks, dtype=jnp.int32), nblk - 1)
    blk_e = jnp.minimum(jnp.searchsorted(ends, blk * tm, side="right"), N_EXPERTS - 1).astype(jnp.int32)
    return dest, slot_tok, blk_e, nblk.reshape(1)


def _moe_block(h, gain, rw, wg, wu, wd, lp, npad, nbatch, lead_rows):
    n, d = h.shape
    xp, route = _router(h, gain, rw)
    experts = route[:, :2].astype(jnp.int32)
    tm = _divisor_tile(n, 1024, GATHER_ROWS) if n % GATHER_ROWS == 0 else n
    rows = min(GATHER_ROWS, tm)
    nblocks = -(-(2 * n + N_EXPERTS * (tm - 1)) // tm)
    dest, slot_tok, blk_e, nblk = _moe_layout(experts, tm, nblocks)
    xg = _moe_gather(xp, slot_tok, rows)
    yg = _moe_ffn(xg, blk_e, nblk, wg, wu, wd, tm)
    return _moe_combine(h, route, dest, yg, rows, lp, npad, nbatch, drop_lead=(lead_rows == rows))


def _inproj_layout(d_model):
    att_w = HEADS * ATT_DH
    qk_w = HEADS * ML_DK
    v_w = HEADS * ML_DV
    names = ("qa", "ka", "va", "fa", "qm", "km", "vm", "im", "fm", "om", "ga", "gm")
    widths = (att_w, att_w, att_w, HEADS, qk_w, qk_w, v_w, HEADS, HEADS, v_w, d_model, d_model)
    src, off = {}, 0
    for nm, wd in zip(names, widths):
        src[nm] = (off, wd)
        off += wd
    big_order = ("vm", "om", "ga", "gm", "qa", "ka", "va", "qm", "km")
    cols, off = {}, 0
    for nm in big_order:
        cols[nm] = off
        off += src[nm][1]
    return src, big_order, cols


def kernel(x, meta_tokens, ln_mix, w_in, att_q_norm, att_k_norm, att_f_bias, ml_conv_w, ml_conv_b, ml_i_bias,
           ml_f_bias, ml_h_norm, w_branch_att, w_branch_ml, w_out, ln_ffn, dense_w_gate, dense_w_up, dense_w_down,
           router_w, moe_w_gate, moe_w_up, moe_w_down):
    nbatch, seq, d = x.shape
    depth = w_in.shape[0]
    lp = -(-(CHUNK + seq) // SEQ_TILE) * SEQ_TILE
    npad = lp - seq - N_META
    n = nbatch * lp
    src, big_order, cols = _inproj_layout(d)

    h = jnp.concatenate([
        jnp.zeros((nbatch, npad, d), x.dtype),
        jnp.broadcast_to(meta_tokens[None].astype(x.dtype), (nbatch, N_META, d)),
        x], axis=1).reshape(n, d)

    for layer in range(depth):
        wl = w_in[layer]
        w_big = jnp.concatenate([wl[:, src[nm][0]:src[nm][0] + src[nm][1]] for nm in big_order], axis=1).astype(BF16)
        w_gates = jnp.concatenate(
            [wl[:, src[nm][0]:src[nm][0] + HEADS] for nm in ("fa", "fm", "im")]
            + [jnp.zeros((d, LANE - 3 * HEADS), F32)], axis=1)
        gate_bias = jnp.concatenate([att_f_bias[layer], ml_f_bias[layer], ml_i_bias[layer],
                                     jnp.zeros((LANE - 3 * HEADS,), F32)]).reshape(1, LANE)

        proj, gates = _norm_inproj(h, ln_mix[layer].reshape(1, d), w_big, w_gates)
        proj3 = proj.reshape(nbatch, lp, -1)
        gcol, grow = _gate_prep(gates.reshape(nbatch, lp, LANE), gate_bias, npad)
        qa, ka, qm, kmt = _qk_prep(proj3, cols, att_q_norm[layer].reshape(1, ATT_DH),
                                   att_k_norm[layer].reshape(1, ATT_DH), ml_conv_w[layer],
                                   ml_conv_b[layer].reshape(1, -1), npad)
        c_rows = grow[:, :HEADS, :].reshape(nbatch * HEADS, lp // SEQ_TILE, 1, SEQ_TILE)
        y_att = _fox_attention(qa, ka, proj3, cols["va"], c_rows, npad)
        y_ml = _mlstm(qm, kmt, proj3, cols, gcol, grow, ml_h_norm[layer].reshape(1, -1))
        merged = _merge(y_att.reshape(n, -1), y_ml.reshape(n, -1), w_branch_att[layer].astype(BF16),
                        w_branch_ml[layer].astype(BF16), proj, cols)
        h = _out_residual(merged, w_out[layer].astype(BF16), h, lp, npad, nbatch)

        j = layer // 2
        gain = ln_ffn[layer].reshape(1, d)
        if layer % 2 == 0:
            h = _ffn(h, gain, dense_w_gate[j].astype(BF16), dense_w_up[j].astype(BF16),
                     dense_w_down[j].astype(BF16), lp, npad, nbatch)
        else:
            rw = jnp.concatenate([router_w[j], jnp.zeros((d, LANE - N_EXPERTS), F32)], axis=1)
            h = _moe_block(h, gain, rw, moe_w_gate[j].astype(BF16), moe_w_up[j].astype(BF16),
                           moe_w_down[j].astype(BF16), lp, npad, nbatch,
                           lead_rows=(lp - seq) if layer == depth - 1 else 0)

    if h.shape[0] == nbatch * seq:
        return h.reshape(nbatch, seq, d)
    return h.reshape(nbatch, lp, d)[:, lp - seq:, :]
```
